```python
import jax, jax.numpy as jnp
from jax import lax
import numpy as np

D_MODEL = 1024
BATCH = 2
SEQ = 8192
DEPTH = 2

CHUNK = 64
EPS = 1e-6
CONV_K = 3
SGU_WIDTH = 512
SGU_GROUPS = 8
SGU_HEAD = SGU_WIDTH // SGU_GROUPS
SGU_BLOCK = 128
CONV_WIDTH = 512
POOL_WIDTH = 512
POOL_WINDOWS = (2, 4, 8, 16)
POOL_GROUPS = len(POOL_WINDOWS)
POOL_HEAD = POOL_WIDTH // POOL_GROUPS
N_BRANCH = 3
IN_COLS = 2 * SGU_WIDTH + 3 * CONV_WIDTH + POOL_WIDTH + N_BRANCH * D_MODEL
D_FF = 2816

kernel_name = "hybrid_gated_gmlp_shortconv_pool_block"


def _rmsnorm(x, g):
    x32 = x.astype(jnp.float32)
    y = x32 * lax.rsqrt(jnp.mean(x32 * x32, axis=-1, keepdims=True) + EPS)
    return (y * g.astype(jnp.float32)).astype(x.dtype)


def _causal_dwconv(x, w):
    k = w.shape[0]
    s = x.shape[1]
    xp = jnp.pad(x, ((0, 0), (k - 1, 0), (0, 0)))
    y = xp[:, 0:s, :] * w[0]
    for i in range(1, k):
        y = y + xp[:, i:i + s, :] * w[i]
    return y


def _sgu(u, v, ln_g, w_s, b_s):
    bn, s, _ = u.shape
    nb = s // SGU_BLOCK
    v = v.reshape(bn, nb, SGU_BLOCK, SGU_GROUPS, SGU_HEAD)
    v32 = v.astype(jnp.float32)
    mu = jnp.mean(v32, axis=-1, keepdims=True)
    var = jnp.mean(jnp.square(v32 - mu), axis=-1, keepdims=True)
    vn = ((v32 - mu) * lax.rsqrt(var + EPS)).astype(u.dtype) * ln_g.reshape(SGU_GROUPS, SGU_HEAD)
    chunk_id = jnp.arange(SGU_BLOCK) // CHUNK
    mask = chunk_id[None, :] <= chunk_id[:, None]
    w = jnp.where(mask[None], w_s, 0)
    sp = jnp.einsum('gij,bnjgc->bnigc', w, vn) + b_s.T[None, None, :, :, None]
    return u * sp.reshape(bn, s, SGU_WIDTH)


def _pool_mixer(z, w_pool, scale):
    bn, s, _ = z.shape
    zg = z.reshape(bn, s, POOL_GROUPS, POOL_HEAD)
    cs = jnp.cumsum(zg.astype(jnp.float32), axis=1)
    t = jnp.arange(s, dtype=jnp.float32)
    outs = []
    for g, win in enumerate(POOL_WINDOWS):
        c = cs[:, :, g]
        prev = jnp.pad(c[:, :s - win], ((0, 0), (win, 0), (0, 0)))
        count = jnp.minimum(t + 1.0, float(win))[None, :, None]
        outs.append((c - prev) / count - zg[:, :, g].astype(jnp.float32))
    pooled = jnp.stack(outs, axis=2).astype(z.dtype)
    y = jnp.einsum('bsgc,gcd->bsgd', pooled, w_pool)
    return y.reshape(bn, s, POOL_WIDTH) * scale


def setup_inputs(seed: int = 0) -> dict:
    key = jax.random.key(seed)
    ks = jax.random.split(key, 20)
    f = jnp.float32
    n = lambda k, shp, sc: jax.random.normal(k, shp, f) * sc
    return {
        "x": n(ks[0], (BATCH, SEQ, D_MODEL), 1.0),
        "norm_mix": 1.0 + n(ks[1], (DEPTH, D_MODEL), 0.02),
        "w_in": n(ks[2], (DEPTH, D_MODEL, IN_COLS), D_MODEL ** -0.5),
        "sgu_ln": 1.0 + n(ks[3], (DEPTH, SGU_WIDTH), 0.02),
        "sgu_w": n(ks[4], (DEPTH, SGU_GROUPS, SGU_BLOCK, SGU_BLOCK), 0.5 * SGU_BLOCK ** -0.5),
        "sgu_b": 1.0 + n(ks[5], (DEPTH, SGU_GROUPS, SGU_BLOCK), 0.02),
        "conv_b_w": n(ks[6], (DEPTH, CONV_K, CONV_WIDTH), CONV_K ** -0.5),
        "pool_w": n(ks[7], (DEPTH, POOL_GROUPS, POOL_HEAD, POOL_HEAD), POOL_HEAD ** -0.5),
        "pool_scale": 1.0 + n(ks[8], (DEPTH, POOL_WIDTH), 0.02),
        "w_br_a": n(ks[9], (DEPTH, SGU_WIDTH, D_MODEL), SGU_WIDTH ** -0.5),
        "w_br_b": n(ks[10], (DEPTH, CONV_WIDTH, D_MODEL), CONV_WIDTH ** -0.5),
        "w_br_c": n(ks[11], (DEPTH, POOL_WIDTH, D_MODEL), POOL_WIDTH ** -0.5),
        "w_o": n(ks[12], (DEPTH, D_MODEL, D_MODEL), D_MODEL ** -0.5),
        "norm_ffn": 1.0 + n(ks[13], (DEPTH, D_MODEL), 0.02),
        "w_up": n(ks[14], (DEPTH, D_MODEL, 2 * D_FF), D_MODEL ** -0.5),
        "ffn_conv_w": n(ks[15], (DEPTH, CONV_K, 2 * D_FF), CONV_K ** -0.5),
        "w_down": n(ks[16], (DEPTH, D_FF, D_MODEL), D_FF ** -0.5),
        "norm_f": 1.0 + n(ks[17], (D_MODEL,), 0.02),
    }


def reference(x, norm_mix, w_in, sgu_ln, sgu_w, sgu_b, conv_b_w, pool_w, pool_scale,
              w_br_a, w_br_b, w_br_c, w_o, norm_ffn, w_up, ffn_conv_w, w_down, norm_f):
    bn, s, _ = x.shape
    splits = np.cumsum([SGU_WIDTH, SGU_WIDTH, CONV_WIDTH, CONV_WIDTH, CONV_WIDTH, POOL_WIDTH]).tolist()
    for l in range(DEPTH):
        h = _rmsnorm(x, norm_mix[l])
        p = h @ w_in[l]
        u, v, bg, cg, xb, zc, gl = jnp.split(p, splits, axis=-1)
        ya = _sgu(jax.nn.gelu(u, approximate=False), jax.nn.gelu(v, approximate=False),
                  sgu_ln[l], sgu_w[l], sgu_b[l])
        yb = bg * _causal_dwconv(cg * xb, conv_b_w[l])
        yc = _pool_mixer(zc, pool_w[l], pool_scale[l])
        g = jax.nn.sigmoid(gl).reshape(bn, s, N_BRANCH, D_MODEL)
        merged = (g[:, :, 0] * (ya @ w_br_a[l])
                  + g[:, :, 1] * (yb @ w_br_b[l])
                  + g[:, :, 2] * (yc @ w_br_c[l]))
        x = x + merged @ w_o[l]
        h = _rmsnorm(x, norm_ffn[l])
        a = _causal_dwconv(h @ w_up[l], ffn_conv_w[l])
        ag, av = jnp.split(a, 2, axis=-1)
        x = x + (jax.nn.silu(ag) * av) @ w_down[l]
    return _rmsnorm(x, norm_f)
```

```python
import functools

import jax
import jax.numpy as jnp
from jax import lax
from jax.experimental import pallas as pl
from jax.experimental.pallas import tpu as pltpu

D_MODEL = 1024
EPS = 1e-6
CHUNK = 64
SGU_WIDTH = 512
SGU_GROUPS = 8
SGU_HEAD = SGU_WIDTH // SGU_GROUPS
SGU_BLOCK = 128
CONV_WIDTH = 512
POOL_WIDTH = 512
POOL_WINDOWS = (2, 4, 8, 16)
POOL_HEAD = POOL_WIDTH // len(POOL_WINDOWS)
D_FF = 2816

OFF_U = 0
OFF_V = OFF_U + SGU_WIDTH
OFF_B = OFF_V + SGU_WIDTH
OFF_C = OFF_B + CONV_WIDTH
OFF_X = OFF_C + CONV_WIDTH
OFF_Z = OFF_X + CONV_WIDTH
OFF_G = OFF_Z + POOL_WIDTH

LANES = 128
SUBLANES = 8
MXU_DIM = 256
SEQ_TILE = 512
FFN_COLS = MXU_DIM
POOL_HIST = 16
VMEM_LIMIT = 56 * 1024 * 1024

F32 = jnp.float32
BF16 = jnp.bfloat16


def _dot(a, b):
    return jnp.dot(a, b, preferred_element_type=F32)


def _rmsnorm(x, g):
    return x * lax.rsqrt(jnp.mean(x * x, axis=-1, keepdims=True) + EPS) * g


def _gelu(x):
    return 0.5 * x * (1.0 + lax.erf(x * (0.5 ** 0.5)))


def _shift_rows(ext, k, hist):
    return pltpu.roll(ext, k, 0)[hist:]


def _mixer_kernel(x_ref, nm_ref, win_ref, ln_ref, avg_ref, sw_ref, sb_ref, cw_ref,
                  pw_ref, ps_ref, wa_ref, wb_ref, wc_ref, wo_ref, o_ref,
                  ya_ref, cx_ref, z_ref):
    ts = x_ref.shape[0]
    t = pl.program_id(1)

    @pl.when(t == 0)
    def _():
        cx_ref[0:SUBLANES, :] = jnp.zeros((SUBLANES, CONV_WIDTH), F32)
        z_ref[0:POOL_HIST, :] = jnp.zeros((POOL_HIST, POOL_WIDTH), F32)

    x = x_ref[...]
    h = _rmsnorm(x, nm_ref[...]).astype(BF16)

    def proj(lo, width):
        return _dot(h, win_ref[:, lo:lo + width])

    u = _gelu(proj(OFF_U, SGU_WIDTH))
    v = _gelu(proj(OFF_V, SGU_WIDTH))
    avg = avg_ref[...]
    d = v - _dot(v.astype(BF16), avg)
    var = _dot((d * d).astype(BF16), avg)
    vn = (d * lax.rsqrt(var + EPS) * ln_ref[...]).astype(BF16)

    row = lax.broadcasted_iota(jnp.int32, (2 * SGU_BLOCK, SGU_BLOCK), 0)
    col = lax.broadcasted_iota(jnp.int32, (2 * SGU_BLOCK, SGU_BLOCK), 1)
    visible = (col // CHUNK) <= ((row % SGU_BLOCK) // CHUNK)
    first_head = lax.broadcasted_iota(jnp.int32, (SGU_BLOCK, LANES), 1) < SGU_HEAD
    for k in range(SGU_WIDTH // LANES):
        w_pair = jnp.where(visible, sw_ref[k], jnp.zeros((), BF16))
        bias = sb_ref[:, k * LANES:(k + 1) * LANES]
        for n in range(ts // SGU_BLOCK):
            rows = slice(n * SGU_BLOCK, (n + 1) * SGU_BLOCK)
            cols = slice(k * LANES, (k + 1) * LANES)
            r = _dot(w_pair, vn[rows, cols])
            sp = jnp.where(first_head, r[:SGU_BLOCK], r[SGU_BLOCK:]) + bias
            ya_ref[rows, cols] = (u[rows, cols] * sp).astype(BF16)

    bg = proj(OFF_B, CONV_WIDTH)
    cx = proj(OFF_C, CONV_WIDTH) * proj(OFF_X, CONV_WIDTH)
    cx_ref[SUBLANES:, :] = cx
    cx_ext = cx_ref[...]
    cw = cw_ref[...]
    conv = (cw[2:3] * cx + cw[1:2] * _shift_rows(cx_ext, 1, SUBLANES)
            + cw[0:1] * _shift_rows(cx_ext, 2, SUBLANES))
    yb = (bg * conv).astype(BF16)
    cx_ref[0:SUBLANES, :] = cx[ts - SUBLANES:]

    z = proj(OFF_Z, POOL_WIDTH)
    z_ref[POOL_HIST:, :] = z
    pos = lax.broadcasted_iota(jnp.int32, (ts, POOL_HEAD), 0) + t * ts + 1
    pooled = []
    for g, win in enumerate(POOL_WINDOWS):
        cols = slice(g * POOL_HEAD, (g + 1) * POOL_HEAD)
        s = z_ref[:, cols]
        span = 1
        while span < win:
            s = s + pltpu.roll(s, span, 0)
            span *= 2
        count = jnp.minimum(pos, win).astype(F32)
        pooled.append(s[POOL_HIST:] / count - z[:, cols])
    z_ref[0:POOL_HIST, :] = z[ts - POOL_HIST:]
    half = 2 * POOL_HEAD
    yc = jnp.concatenate(
        [_dot(jnp.concatenate(pooled[0:2], axis=1).astype(BF16), pw_ref[0]),
         _dot(jnp.concatenate(pooled[2:4], axis=1).astype(BF16), pw_ref[1])], axis=1)
    assert yc.shape[1] == 2 * half
    yc = (yc * ps_ref[...]).astype(BF16)

    def gate(i):
        return jax.nn.sigmoid(proj(OFF_G + i * D_MODEL, D_MODEL))

    merged = gate(0) * _dot(ya_ref[...], wa_ref[...])
    merged = merged + gate(1) * _dot(yb, wb_ref[...])
    merged = merged + gate(2) * _dot(yc, wc_ref[...])
    o_ref[...] = x + _dot(merged.astype(BF16), wo_ref[...])


def _ffn_kernel(x_ref, nf_ref, wup_ref, cw_ref, wdn_ref, nfin_ref, o_ref,
                act_ref, carry_ref, *, final_norm):
    ts = x_ref.shape[0]
    t = pl.program_id(1)

    @pl.when(t == 0)
    def _():
        carry_ref[...] = jnp.zeros(carry_ref.shape, F32)

    x = x_ref[...]
    h = _rmsnorm(x, nf_ref[...]).astype(BF16)

    def conv_cols(lo):
        cols = slice(lo, lo + FFN_COLS)
        pre = _dot(h, wup_ref[:, cols])
        ext = jnp.concatenate([carry_ref[:, cols], pre], axis=0)
        cw = cw_ref[:, cols]
        carry_ref[:, cols] = pre[ts - SUBLANES:]
        return (cw[2:3] * pre + cw[1:2] * _shift_rows(ext, 1, SUBLANES)
                + cw[0:1] * _shift_rows(ext, 2, SUBLANES))

    for c in range(D_FF // FFN_COLS):
        ag = conv_cols(c * FFN_COLS)
        av = conv_cols(D_FF + c * FFN_COLS)
        act_ref[:, c * FFN_COLS:(c + 1) * FFN_COLS] = (ag * jax.nn.sigmoid(ag) * av).astype(BF16)

    y = x + _dot(act_ref[...], wdn_ref[...])
    if final_norm:
        y = _rmsnorm(y, nfin_ref[...])
    o_ref[...] = y


def _resident(shape):
    return pl.BlockSpec(shape, lambda b, t: (0,) * len(shape), pipeline_mode=pl.Buffered(1))


def _token_spec(ts):
    return pl.BlockSpec((None, ts, D_MODEL), lambda b, t: (b, t, 0))


def _compiler_params():
    return pltpu.CompilerParams(dimension_semantics=("arbitrary", "arbitrary"),
                                vmem_limit_bytes=VMEM_LIMIT)


def _mixer(x, nm, w_in, ln, avg, sw, sb, cw, pw, ps, wa, wb, wc, wo):
    bn, s, _ = x.shape
    ts = SEQ_TILE
    consts = (nm, w_in, ln, avg, sw, sb, cw, pw, ps, wa, wb, wc, wo)
    return pl.pallas_call(
        _mixer_kernel,
        grid=(bn, s // ts),
        in_specs=[_token_spec(ts)] + [_resident(c.shape) for c in consts],
        out_specs=_token_spec(ts),
        out_shape=jax.ShapeDtypeStruct(x.shape, F32),
        scratch_shapes=[pltpu.VMEM((ts, SGU_WIDTH), BF16),
                        pltpu.VMEM((SUBLANES + ts, CONV_WIDTH), F32),
                        pltpu.VMEM((POOL_HIST + ts, POOL_WIDTH), F32)],
        compiler_params=_compiler_params(),
        name="mixer",
    )(x, *consts)


def _ffn(x, nf, wup, cw, wdn, nfin, final_norm):
    bn, s, _ = x.shape
    ts = SEQ_TILE
    consts = (nf, wup, cw, wdn, nfin)
    return pl.pallas_call(
        functools.partial(_ffn_kernel, final_norm=final_norm),
        grid=(bn, s // ts),
        in_specs=[_token_spec(ts)] + [_resident(c.shape) for c in consts],
        out_specs=_token_spec(ts),
        out_shape=jax.ShapeDtypeStruct(x.shape, F32),
        scratch_shapes=[pltpu.VMEM((ts, D_FF), BF16),
                        pltpu.VMEM((SUBLANES, 2 * D_FF), F32)],
        compiler_params=_compiler_params(),
        name="ffn",
    )(x, *consts)


def kernel(x, norm_mix, w_in, sgu_ln, sgu_w, sgu_b, conv_b_w, pool_w, pool_scale,
           w_br_a, w_br_b, w_br_c, w_o, norm_ffn, w_up, ffn_conv_w, w_down, norm_f):
    depth = w_in.shape[0]
    assert x.shape[1] % SEQ_TILE == 0 and SEQ_TILE % SGU_BLOCK == 0
    assert D_FF % FFN_COLS == 0

    head = jnp.arange(SGU_WIDTH) // SGU_HEAD
    avg = jnp.where(head[:, None] == head[None, :], 1.0 / SGU_HEAD, 0.0).astype(BF16)
    zero_blk = jnp.zeros((POOL_HEAD, POOL_HEAD), BF16)

    for l in range(depth):
        sw = sgu_w[l].astype(BF16).reshape(SGU_GROUPS // 2, 2 * SGU_BLOCK, SGU_BLOCK)
        sb = jnp.repeat(sgu_b[l].T, SGU_HEAD, axis=1)
        p = pool_w[l].astype(BF16)
        pw = jnp.stack([jnp.block([[p[0], zero_blk], [zero_blk, p[1]]]),
                        jnp.block([[p[2], zero_blk], [zero_blk, p[3]]])])
        x = _mixer(x, norm_mix[l][None], w_in[l].astype(BF16), sgu_ln[l][None], avg, sw, sb,
                   conv_b_w[l], pw, pool_scale[l][None],
                   w_br_a[l].astype(BF16), w_br_b[l].astype(BF16), w_br_c[l].astype(BF16),
                   w_o[l].astype(BF16))
        x = _ffn(x, norm_ffn[l][None], w_up[l].astype(BF16), ffn_conv_w[l],
                 w_down[l].astype(BF16), norm_f[None], final_norm=(l == depth - 1))
    return x
```

```python
import functools

import jax
import jax.numpy as jnp
from jax import lax
from jax.experimental import pallas as pl
from jax.experimental.pallas import tpu as pltpu

D_MODEL = 1024
EPS = 1e-6
CHUNK = 64
SGU_WIDTH = 512
SGU_GROUPS = 8
SGU_HEAD = SGU_WIDTH // SGU_GROUPS
SGU_BLOCK = 128
CONV_WIDTH = 512
POOL_WIDTH = 512
POOL_WINDOWS = (2, 4, 8, 16)
POOL_HEAD = POOL_WIDTH // len(POOL_WINDOWS)
D_FF = 2816

OFF_U = 0
OFF_V = OFF_U + SGU_WIDTH
OFF_B = OFF_V + SGU_WIDTH
OFF_C = OFF_B + CONV_WIDTH
OFF_X = OFF_C + CONV_WIDTH
OFF_Z = OFF_X + CONV_WIDTH
OFF_G = OFF_Z + POOL_WIDTH

LANES = 128
SUBLANES = 8
MXU_DIM = 256
SEQ_TILE = 512
FFN_COLS = MXU_DIM
POOL_HIST = 16
VMEM_LIMIT = 56 * 1024 * 1024

F32 = jnp.float32
BF16 = jnp.bfloat16


def _dot(a, b):
    return jnp.dot(a, b, preferred_element_type=F32)


def _rmsnorm(x, g):
    return x * lax.rsqrt(jnp.mean(x * x, axis=-1, keepdims=True) + EPS) * g


def _gelu(x):
    return 0.5 * x * (1.0 + lax.erf(x * (0.5 ** 0.5)))


def _shift_rows(ext, k, hist):
    return pltpu.roll(ext, k, 0)[hist:]


def _mixer_kernel(x_ref, nm_ref, win_ref, ln_ref, sw_ref, sb_ref, cw_ref,
                  pw_ref, ps_ref, wa_ref, wb_ref, wc_ref, wo_ref, o_ref,
                  ya_ref, cx_ref, z_ref):
    ts = x_ref.shape[0]
    t = pl.program_id(1)

    @pl.when(t == 0)
    def _():
        cx_ref[0:SUBLANES, :] = jnp.zeros((SUBLANES, CONV_WIDTH), F32)
        z_ref[0:POOL_HIST, :] = jnp.zeros((POOL_HIST, POOL_WIDTH), F32)

    x = x_ref[...]
    h = _rmsnorm(x, nm_ref[...]).astype(BF16)

    def proj(lo, width):
        return _dot(h, win_ref[:, lo:lo + width])

    first_head = lax.broadcasted_iota(jnp.int32, (1, LANES), 1) < SGU_HEAD
    sign = jnp.where(first_head, 1.0, -1.0)
    slabs = [slice(k * LANES, (k + 1) * LANES) for k in range(SGU_WIDTH // LANES)]

    def head_sums(a):
        return (jnp.sum(a, axis=-1, keepdims=True), jnp.sum(a * sign, axis=-1, keepdims=True))

    def head_mean(sums):
        return (sums[0] + sign * sums[1]) * (0.5 / SGU_HEAD)

    v = _gelu(proj(OFF_V, SGU_WIDTH))
    v_sums = [(head_sums(v[:, c]), head_sums(v[:, c] * v[:, c])) for c in slabs]

    z = proj(OFF_Z, POOL_WIDTH)
    z_ref[POOL_HIST:, :] = z
    pos = lax.broadcasted_iota(jnp.int32, (ts, POOL_HEAD), 0) + t * ts + 1

    def pool(g):
        win = POOL_WINDOWS[g]
        cols = slice(g * POOL_HEAD, (g + 1) * POOL_HEAD)
        s = z_ref[:, cols]
        span = 1
        while span < win:
            s = s + pltpu.roll(s, span, 0)
            span *= 2
        count = jnp.minimum(pos, win).astype(F32)
        return s[POOL_HIST:] / count - z[:, cols]

    bg = proj(OFF_B, CONV_WIDTH)
    pooled = [pool(0), pool(1)]
    cg = proj(OFF_C, CONV_WIDTH)
    pooled.append(pool(2))
    cx = cg * proj(OFF_X, CONV_WIDTH)
    pooled.append(pool(3))
    z_ref[0:POOL_HIST, :] = z[ts - POOL_HIST:]
    cx_ref[SUBLANES:, :] = cx
    cx_ext = cx_ref[...]
    cw = cw_ref[...]
    conv = (cw[2:3] * cx + cw[1:2] * _shift_rows(cx_ext, 1, SUBLANES)
            + cw[0:1] * _shift_rows(cx_ext, 2, SUBLANES))
    yb = (bg * conv).astype(BF16)
    cx_ref[0:SUBLANES, :] = cx[ts - SUBLANES:]

    u = _gelu(proj(OFF_U, SGU_WIDTH))
    row = lax.broadcasted_iota(jnp.int32, (SGU_BLOCK, 2 * SGU_BLOCK), 0)
    col = lax.broadcasted_iota(jnp.int32, (SGU_BLOCK, 2 * SGU_BLOCK), 1)
    visible = ((col % SGU_BLOCK) // CHUNK) <= (row // CHUNK)
    zero = jnp.zeros((), BF16)
    blocks = [slice(n * SGU_BLOCK, (n + 1) * SGU_BLOCK) for n in range(ts // SGU_BLOCK)]
    for k, cols in enumerate(slabs):
        mean = head_mean(v_sums[k][0])
        var = head_mean(v_sums[k][1]) - mean * mean
        vn = ((v[:, cols] - mean) * lax.rsqrt(var + EPS) * ln_ref[:, cols]).astype(BF16)
        top = jnp.where(first_head, vn, zero)
        bot = jnp.where(first_head, zero, vn)
        rhs = jnp.concatenate(
            [jnp.concatenate([top[rows], bot[rows]], axis=0) for rows in blocks], axis=1)
        w_pair = jnp.where(visible, sw_ref[k], zero)
        r = _dot(w_pair, rhs)
        bias = sb_ref[:, cols]
        for n, rows in enumerate(blocks):
            sp = r[:, n * LANES:(n + 1) * LANES] + bias
            ya_ref[rows, cols] = (u[rows, cols] * sp).astype(BF16)

    yc = jnp.concatenate(
        [_dot(jnp.concatenate(pooled[0:2], axis=1).astype(BF16), pw_ref[0]),
         _dot(jnp.concatenate(pooled[2:4], axis=1).astype(BF16), pw_ref[1])], axis=1)
    yc = (yc * ps_ref[...]).astype(BF16)

    def gate(i):
        return jax.nn.sigmoid(proj(OFF_G + i * D_MODEL, D_MODEL))

    merged = gate(0) * _dot(ya_ref[...], wa_ref[...])
    merged = merged + gate(1) * _dot(yb, wb_ref[...])
    merged = merged + gate(2) * _dot(yc, wc_ref[...])
    o_ref[...] = x + _dot(merged.astype(BF16), wo_ref[...])


def _ffn_kernel(x_ref, nf_ref, wup_ref, cw_ref, wdn_ref, nfin_ref, o_ref,
                act_ref, carry_ref, *, final_norm):
    ts = x_ref.shape[0]
    t = pl.program_id(1)

    @pl.when(t == 0)
    def _():
        carry_ref[...] = jnp.zeros(carry_ref.shape, F32)

    x = x_ref[...]
    h = _rmsnorm(x, nf_ref[...]).astype(BF16)

    def conv_cols(lo):
        cols = slice(lo, lo + FFN_COLS)
        pre = _dot(h, wup_ref[:, cols])
        ext = jnp.concatenate([carry_ref[:, cols], pre], axis=0)
        cw = cw_ref[:, cols]
        carry_ref[:, cols] = pre[ts - SUBLANES:]
        return (cw[2:3] * pre + cw[1:2] * _shift_rows(ext, 1, SUBLANES)
                + cw[0:1] * _shift_rows(ext, 2, SUBLANES))

    for c in range(D_FF // FFN_COLS):
        ag = conv_cols(c * FFN_COLS)
        av = conv_cols(D_FF + c * FFN_COLS)
        act_ref[:, c * FFN_COLS:(c + 1) * FFN_COLS] = (ag * jax.nn.sigmoid(ag) * av).astype(BF16)

    y = x + _dot(act_ref[...], wdn_ref[...])
    if final_norm:
        y = _rmsnorm(y, nfin_ref[...])
    o_ref[...] = y


def _resident(shape):
    return pl.BlockSpec(shape, lambda b, t: (0,) * len(shape), pipeline_mode=pl.Buffered(1))


def _token_spec(ts):
    return pl.BlockSpec((None, ts, D_MODEL), lambda b, t: (b, t, 0))


def _compiler_params():
    return pltpu.CompilerParams(dimension_semantics=("arbitrary", "arbitrary"),
                                vmem_limit_bytes=VMEM_LIMIT)


def _mixer(x, nm, w_in, ln, sw, sb, cw, pw, ps, wa, wb, wc, wo):
    bn, s, _ = x.shape
    ts = SEQ_TILE
    consts = (nm, w_in, ln, sw, sb, cw, pw, ps, wa, wb, wc, wo)
    return pl.pallas_call(
        _mixer_kernel,
        grid=(bn, s // ts),
        in_specs=[_token_spec(ts)] + [_resident(c.shape) for c in consts],
        out_specs=_token_spec(ts),
        out_shape=jax.ShapeDtypeStruct(x.shape, F32),
        scratch_shapes=[pltpu.VMEM((ts, SGU_WIDTH), BF16),
                        pltpu.VMEM((SUBLANES + ts, CONV_WIDTH), F32),
                        pltpu.VMEM((POOL_HIST + ts, POOL_WIDTH), F32)],
        compiler_params=_compiler_params(),
        name="mixer",
    )(x, *consts)


def _ffn(x, nf, wup, cw, wdn, nfin, final_norm):
    bn, s, _ = x.shape
    ts = SEQ_TILE
    consts = (nf, wup, cw, wdn, nfin)
    return pl.pallas_call(
        functools.partial(_ffn_kernel, final_norm=final_norm),
        grid=(bn, s // ts),
        in_specs=[_token_spec(ts)] + [_resident(c.shape) for c in consts],
        out_specs=_token_spec(ts),
        out_shape=jax.ShapeDtypeStruct(x.shape, F32),
        scratch_shapes=[pltpu.VMEM((ts, D_FF), BF16),
                        pltpu.VMEM((SUBLANES, 2 * D_FF), F32)],
        compiler_params=_compiler_params(),
        name="ffn",
    )(x, *consts)


def kernel(x, norm_mix, w_in, sgu_ln, sgu_w, sgu_b, conv_b_w, pool_w, pool_scale,
           w_br_a, w_br_b, w_br_c, w_o, norm_ffn, w_up, ffn_conv_w, w_down, norm_f):
    depth = w_in.shape[0]
    assert x.shape[1] % SEQ_TILE == 0 and SEQ_TILE % SGU_BLOCK == 0
    assert D_FF % FFN_COLS == 0

    zero_blk = jnp.zeros((POOL_HEAD, POOL_HEAD), BF16)

    for l in range(depth):
        sw = (sgu_w[l].astype(BF16).reshape(SGU_GROUPS // 2, 2, SGU_BLOCK, SGU_BLOCK)
              .transpose(0, 2, 1, 3).reshape(SGU_GROUPS // 2, SGU_BLOCK, 2 * SGU_BLOCK))
        sb = jnp.repeat(sgu_b[l].T, SGU_HEAD, axis=1)
        p = pool_w[l].astype(BF16)
        pw = jnp.stack([jnp.block([[p[0], zero_blk], [zero_blk, p[1]]]),
                        jnp.block([[p[2], zero_blk], [zero_blk, p[3]]])])
        x = _mixer(x, norm_mix[l][None], w_in[l].astype(BF16), sgu_ln[l][None], sw, sb,
                   conv_b_w[l], pw, pool_scale[l][None],
                   w_br_a[l].astype(BF16), w_br_b[l].astype(BF16), w_br_c[l].astype(BF16),
                   w_o[l].astype(BF16))
        x = _ffn(x, norm_ffn[l][None], w_up[l].astype(BF16), ffn_conv_w[l],
                 w_down[l].astype(BF16), norm_f[None], final_norm=(l == depth - 1))
    return x
```

```python
import functools

import jax
import jax.numpy as jnp
from jax import lax
from jax.experimental import pallas as pl
from jax.experimental.pallas import tpu as pltpu

D_MODEL = 1024
EPS = 1e-6
CHUNK = 64
SGU_WIDTH = 512
SGU_GROUPS = 8
SGU_HEAD = SGU_WIDTH // SGU_GROUPS
SGU_BLOCK = 128
CONV_WIDTH = 512
POOL_WIDTH = 512
POOL_WINDOWS = (2, 4, 8, 16)
POOL_HEAD = POOL_WIDTH // len(POOL_WINDOWS)
D_FF = 2816

OFF_U = 0
OFF_V = OFF_U + SGU_WIDTH
OFF_B = OFF_V + SGU_WIDTH
OFF_C = OFF_B + CONV_WIDTH
OFF_X = OFF_C + CONV_WIDTH
OFF_Z = OFF_X + CONV_WIDTH
OFF_G = OFF_Z + POOL_WIDTH

LANES = 128
SUBLANES = 8
MXU_DIM = 256
SEQ_TILE = 512
FFN_COLS = MXU_DIM
POOL_HIST = 16
VMEM_LIMIT = 56 * 1024 * 1024

F32 = jnp.float32
BF16 = jnp.bfloat16


def _dot(a, b):
    return jnp.dot(a, b, preferred_element_type=F32)


def _rmsnorm(x, g):
    return x * lax.rsqrt(jnp.mean(x * x, axis=-1, keepdims=True) + EPS) * g


def _gelu(x):
    return 0.5 * x * (1.0 + lax.erf(x * (0.5 ** 0.5)))


def _shift_rows(ext, k, hist):
    return pltpu.roll(ext, k, 0)[hist:]


def _mixer_kernel(x_ref, nm_ref, win_ref, ln_ref, sw_ref, sb_ref, cw_ref,
                  pw_ref, ps_ref, wa_ref, wb_ref, wc_ref, wo_ref, o_ref,
                  ya_ref, cx_ref, z_ref):
    ts = x_ref.shape[0]
    t = pl.program_id(1)

    @pl.when(t == 0)
    def _():
        cx_ref[0:SUBLANES, :] = jnp.zeros((SUBLANES, CONV_WIDTH), F32)
        z_ref[0:POOL_HIST, :] = jnp.zeros((POOL_HIST, POOL_WIDTH), F32)

    x = x_ref[...]
    h = _rmsnorm(x, nm_ref[...]).astype(BF16)

    def proj(lo, width):
        return _dot(h, win_ref[:, lo:lo + width])

    first_head = lax.broadcasted_iota(jnp.int32, (1, LANES), 1) < SGU_HEAD
    sign = jnp.where(first_head, 1.0, -1.0)
    slabs = [slice(k * LANES, (k + 1) * LANES) for k in range(SGU_WIDTH // LANES)]

    def head_sums(a):
        return (jnp.sum(a, axis=-1, keepdims=True), jnp.sum(a * sign, axis=-1, keepdims=True))

    def head_mean(sums):
        return (sums[0] + sign * sums[1]) * (0.5 / SGU_HEAD)

    v = _gelu(proj(OFF_V, SGU_WIDTH))
    v_sums = [(head_sums(v[:, c]), head_sums(v[:, c] * v[:, c])) for c in slabs]

    z = proj(OFF_Z, POOL_WIDTH)
    z_ref[POOL_HIST:, :] = z
    pos = lax.broadcasted_iota(jnp.int32, (ts, POOL_HEAD), 0) + t * ts + 1

    def pool(g):
        win = POOL_WINDOWS[g]
        cols = slice(g * POOL_HEAD, (g + 1) * POOL_HEAD)
        s = z_ref[:, cols]
        span = 1
        while span < win:
            s = s + pltpu.roll(s, span, 0)
            span *= 2
        count = jnp.minimum(pos, win).astype(F32)
        return s[POOL_HIST:] / count - z[:, cols]

    bg = proj(OFF_B, CONV_WIDTH)
    pooled = [pool(0), pool(1)]
    cg = proj(OFF_C, CONV_WIDTH)
    pooled.append(pool(2))
    cx = cg * proj(OFF_X, CONV_WIDTH)
    pooled.append(pool(3))
    z_ref[0:POOL_HIST, :] = z[ts - POOL_HIST:]
    cx_ref[SUBLANES:, :] = cx
    cx_ext = cx_ref[...]
    cw = cw_ref[...]
    conv = (cw[2:3] * cx + cw[1:2] * _shift_rows(cx_ext, 1, SUBLANES)
            + cw[0:1] * _shift_rows(cx_ext, 2, SUBLANES))
    yb = (bg * conv).astype(BF16)
    cx_ref[0:SUBLANES, :] = cx[ts - SUBLANES:]

    u = _gelu(proj(OFF_U, SGU_WIDTH))
    row = lax.broadcasted_iota(jnp.int32, (SGU_BLOCK, 2 * SGU_BLOCK), 0)
    col = lax.broadcasted_iota(jnp.int32, (SGU_BLOCK, 2 * SGU_BLOCK), 1)
    visible = ((col % SGU_BLOCK) // CHUNK) <= (row // CHUNK)
    zero = jnp.zeros((), BF16)
    blocks = [slice(n * SGU_BLOCK, (n + 1) * SGU_BLOCK) for n in range(ts // SGU_BLOCK)]
    for k, cols in enumerate(slabs):
        mean = head_mean(v_sums[k][0])
        var = head_mean(v_sums[k][1]) - mean * mean
        vn = ((v[:, cols] - mean) * lax.rsqrt(var + EPS) * ln_ref[:, cols]).astype(BF16)
        top = jnp.where(first_head, vn, zero)
        bot = jnp.where(first_head, zero, vn)
        rhs = jnp.concatenate(
            [jnp.concatenate([top[rows], bot[rows]], axis=0) for rows in blocks], axis=1)
        w_pair = jnp.where(visible, sw_ref[k], zero)
        r = _dot(w_pair, rhs)
        bias = sb_ref[:, cols]
        for n, rows in enumerate(blocks):
            sp = r[:, n * LANES:(n + 1) * LANES] + bias
            ya_ref[rows, cols] = (u[rows, cols] * sp).astype(BF16)

    yc = jnp.concatenate(
        [_dot(jnp.concatenate(pooled[0:2], axis=1).astype(BF16), pw_ref[0]),
         _dot(jnp.concatenate(pooled[2:4], axis=1).astype(BF16), pw_ref[1])], axis=1)
    yc = (yc * ps_ref[...]).astype(BF16)

    def gate(i):
        return jax.nn.sigmoid(proj(OFF_G + i * D_MODEL, D_MODEL))

    merged = gate(0) * _dot(ya_ref[...], wa_ref[...])
    merged = merged + gate(1) * _dot(yb, wb_ref[...])
    merged = merged + gate(2) * _dot(yc, wc_ref[...])
    o_ref[...] = x + _dot(merged.astype(BF16), wo_ref[...])


def _ffn_kernel(x_ref, nf_ref, wup_ref, cw_ref, wdn_ref, nfin_ref, o_ref,
                act_ref, carry_ref, *, final_norm):
    ts = x_ref.shape[0]
    t = pl.program_id(1)

    @pl.when(t == 0)
    def _():
        carry_ref[...] = jnp.zeros(carry_ref.shape, F32)

    x = x_ref[...]
    h = _rmsnorm(x, nf_ref[...]).astype(BF16)

    def conv_cols(lo):
        cols = slice(lo, lo + FFN_COLS)
        pre = _dot(h, wup_ref[:, cols])
        ext = jnp.concatenate([carry_ref[:, cols], pre], axis=0)
        cw = cw_ref[:, cols]
        carry_ref[:, cols] = pre[ts - SUBLANES:]
        return (cw[2:3] * pre + cw[1:2] * _shift_rows(ext, 1, SUBLANES)
                + cw[0:1] * _shift_rows(ext, 2, SUBLANES))

    for c in range(D_FF // FFN_COLS):
        ag = conv_cols(c * FFN_COLS)
        av = conv_cols(D_FF + c * FFN_COLS)
        act_ref[:, c * FFN_COLS:(c + 1) * FFN_COLS] = (ag * jax.nn.sigmoid(ag) * av).astype(BF16)

    y = x + _dot(act_ref[...], wdn_ref[...])
    if final_norm:
        y = _rmsnorm(y, nfin_ref[...])
    o_ref[...] = y


def _token_spec(ts):
    return pl.BlockSpec((None, ts, D_MODEL), lambda b, t: (b, t, 0))


def _compiler_params():
    return pltpu.CompilerParams(dimension_semantics=("arbitrary", "arbitrary"),
                                vmem_limit_bytes=VMEM_LIMIT)


def _with_casts(body, n_consts, n_casts, *refs):
    x_ref, consts = refs[0], refs[1:1 + n_consts]
    cast_in = refs[1 + n_consts:1 + n_consts + n_casts]
    o_ref = refs[1 + n_consts + n_casts]
    cast_out = refs[2 + n_consts + n_casts:2 + n_consts + 2 * n_casts]
    scratch = refs[2 + n_consts + 2 * n_casts:]
    body(x_ref, *consts, o_ref, *scratch)
    for src, dst in zip(cast_in, cast_out):
        dst[...] = src[...].astype(BF16)


def _layer_call(body, name, x, consts, casts, scratch_shapes):
    bn, s, _ = x.shape
    ts = SEQ_TILE
    nt = s // ts
    steps = bn * nt

    in_specs = [_token_spec(ts)]
    for a, l in consts:
        if l is None:
            in_specs.append(pl.BlockSpec(a.shape, lambda b, t, n=a.ndim: (0,) * n,
                                         pipeline_mode=pl.Buffered(1)))
        else:
            in_specs.append(pl.BlockSpec((None,) + a.shape[1:],
                                         lambda b, t, l=l, n=a.ndim - 1: (l,) + (0,) * n,
                                         pipeline_mode=pl.Buffered(1)))
    out_specs = [_token_spec(ts)]
    out_shape = [jax.ShapeDtypeStruct(x.shape, F32)]
    for w, l in casts:
        _, rows, cols = w.shape
        assert rows % steps == 0
        chunk = rows // steps
        in_specs.append(pl.BlockSpec((None, chunk, cols), lambda b, t, l=l: (l, b * nt + t, 0)))
        out_specs.append(pl.BlockSpec((chunk, cols), lambda b, t: (b * nt + t, 0)))
        out_shape.append(jax.ShapeDtypeStruct((rows, cols), BF16))

    return pl.pallas_call(
        functools.partial(_with_casts, body, len(consts), len(casts)),
        grid=(bn, nt),
        in_specs=in_specs,
        out_specs=out_specs,
        out_shape=out_shape,
        scratch_shapes=scratch_shapes,
        compiler_params=_compiler_params(),
        name=name,
    )(x, *[a for a, _ in consts], *[w for w, _ in casts])


def kernel(x, norm_mix, w_in, sgu_ln, sgu_w, sgu_b, conv_b_w, pool_w, pool_scale,
           w_br_a, w_br_b, w_br_c, w_o, norm_ffn, w_up, ffn_conv_w, w_down, norm_f):
    depth = w_in.shape[0]
    ts = SEQ_TILE
    assert x.shape[1] % ts == 0 and ts % SGU_BLOCK == 0
    assert D_FF % FFN_COLS == 0

    row = lambda a: a[:, None, :]
    sw = (sgu_w.astype(BF16).reshape(depth, SGU_GROUPS // 2, 2, SGU_BLOCK, SGU_BLOCK)
          .transpose(0, 1, 3, 2, 4).reshape(depth, SGU_GROUPS // 2, SGU_BLOCK, 2 * SGU_BLOCK))
    sb = jnp.repeat(sgu_b.transpose(0, 2, 1), SGU_HEAD, axis=2)
    p = pool_w.astype(BF16).reshape(depth, 2, 2, POOL_HEAD, POOL_HEAD)
    eye = jnp.eye(2, dtype=BF16)
    pw = jnp.einsum('lhacd,ab->lhacbd', p, eye).reshape(depth, 2, 2 * POOL_HEAD, 2 * POOL_HEAD)

    mixer_scratch = [pltpu.VMEM((ts, SGU_WIDTH), BF16),
                     pltpu.VMEM((SUBLANES + ts, CONV_WIDTH), F32),
                     pltpu.VMEM((POOL_HIST + ts, POOL_WIDTH), F32)]
    ffn_scratch = [pltpu.VMEM((ts, D_FF), BF16),
                   pltpu.VMEM((SUBLANES, 2 * D_FF), F32)]
    mixer_weights = (w_in, w_br_a, w_br_b, w_br_c, w_o)

    wi, wa, wb, wc, wo = [w[0].astype(BF16) for w in mixer_weights]
    for l in range(depth):
        last = l == depth - 1
        x, wup, wdn = _layer_call(
            _mixer_kernel, "mixer", x,
            [(row(norm_mix), l), (wi, None), (row(sgu_ln), l), (sw, l), (sb, l), (conv_b_w, l),
             (pw, l), (row(pool_scale), l), (wa, None), (wb, None), (wc, None), (wo, None)],
            [(w_up, l), (w_down, l)], mixer_scratch)
        x, *next_weights = _layer_call(
            functools.partial(_ffn_kernel, final_norm=last), "ffn", x,
            [(row(norm_ffn), l), (wup, None), (ffn_conv_w, l), (wdn, None), (norm_f[None], None)],
            [] if last else [(w, l + 1) for w in mixer_weights], ffn_scratch)
        if not last:
            wi, wa, wb, wc, wo = next_weights
    return x
```

```python
import functools

import jax
import jax.numpy as jnp
from jax import lax
from jax.experimental import pallas as pl
from jax.experimental.pallas import tpu as pltpu

D_MODEL = 1024
EPS = 1e-6
CHUNK = 64
SGU_WIDTH = 512
SGU_GROUPS = 8
SGU_HEAD = SGU_WIDTH // SGU_GROUPS
SGU_BLOCK = 128
CONV_WIDTH = 512
POOL_WIDTH = 512
POOL_WINDOWS = (2, 4, 8, 16)
POOL_HEAD = POOL_WIDTH // len(POOL_WINDOWS)
D_FF = 2816

OFF_U = 0
OFF_V = OFF_U + SGU_WIDTH
OFF_B = OFF_V + SGU_WIDTH
OFF_C = OFF_B + CONV_WIDTH
OFF_X = OFF_C + CONV_WIDTH
OFF_Z = OFF_X + CONV_WIDTH
OFF_G = OFF_Z + POOL_WIDTH

LANES = 128
SUBLANES = 8
MXU_DIM = 256
SEQ_TILE = 512
SUBTILES = 1
FFN_COLS = MXU_DIM
SHIFT_BUFS = 4
POOL_HIST = 16
VMEM_LIMIT = 56 * 1024 * 1024

F32 = jnp.float32
BF16 = jnp.bfloat16


def _dot(a, b):
    return jnp.dot(a, b, preferred_element_type=F32)


def _rmsnorm(x, g):
    return x * lax.rsqrt(jnp.mean(x * x, axis=-1, keepdims=True) + EPS) * g


def _gelu(x):
    return 0.5 * x * (1.0 + lax.erf(x * (0.5 ** 0.5)))


def _shift_rows(ext, k, hist):
    return pltpu.roll(ext, k, 0)[hist:]


def _tiles(x_ref):
    ts = SEQ_TILE
    return [slice(s * ts, (s + 1) * ts) for s in range(x_ref.shape[0] // ts)]


def _mixer_kernel(x_ref, nm_ref, win_ref, ln_ref, sw_ref, sb_ref, cw_ref,
                  pw_ref, ps_ref, wa_ref, wb_ref, wc_ref, wo_ref, o_ref,
                  ya_ref, cx_ref, z_ref):
    @pl.when(pl.program_id(1) == 0)
    def _():
        cx_ref[0:SUBLANES, :] = jnp.zeros((SUBLANES, CONV_WIDTH), F32)
        z_ref[0:POOL_HIST, :] = jnp.zeros((POOL_HIST, POOL_WIDTH), F32)

    tiles = _tiles(x_ref)
    hs = [_rmsnorm(x_ref[rows, :], nm_ref[...]).astype(BF16) for rows in tiles]
    for s, rows in enumerate(tiles):
        tile_index = pl.program_id(1) * len(tiles) + s
        o_ref[rows, :] = _mixer_tile(
            x_ref[rows, :], hs[s], tile_index, win_ref, ln_ref, sw_ref, sb_ref, cw_ref, pw_ref,
            ps_ref, wa_ref, wb_ref, wc_ref, wo_ref, ya_ref.at[s], cx_ref, z_ref)


def _mixer_tile(x, h, tile_index, win_ref, ln_ref, sw_ref, sb_ref, cw_ref, pw_ref, ps_ref,
                wa_ref, wb_ref, wc_ref, wo_ref, ya_ref, cx_ref, z_ref):
    ts = x.shape[0]

    def proj(lo, width):
        return _dot(h, win_ref[:, lo:lo + width])

    first_head = lax.broadcasted_iota(jnp.int32, (1, LANES), 1) < SGU_HEAD
    sign = jnp.where(first_head, 1.0, -1.0)
    slabs = [slice(k * LANES, (k + 1) * LANES) for k in range(SGU_WIDTH // LANES)]

    def head_sums(a):
        return (jnp.sum(a, axis=-1, keepdims=True), jnp.sum(a * sign, axis=-1, keepdims=True))

    def head_mean(sums):
        return (sums[0] + sign * sums[1]) * (0.5 / SGU_HEAD)

    v = _gelu(proj(OFF_V, SGU_WIDTH))
    v_sums = [(head_sums(v[:, c]), head_sums(v[:, c] * v[:, c])) for c in slabs]

    z = proj(OFF_Z, POOL_WIDTH)
    z_ref[POOL_HIST:, :] = z
    pos = lax.broadcasted_iota(jnp.int32, (ts, POOL_HEAD), 0) + tile_index * ts + 1

    def pool(g):
        win = POOL_WINDOWS[g]
        cols = slice(g * POOL_HEAD, (g + 1) * POOL_HEAD)
        s = z_ref[:, cols]
        span = 1
        while span < win:
            s = s + pltpu.roll(s, span, 0)
            span *= 2
        count = jnp.minimum(pos, win).astype(F32)
        return s[POOL_HIST:] / count - z[:, cols]

    bg = proj(OFF_B, CONV_WIDTH)
    pooled = [pool(0), pool(1)]
    cg = proj(OFF_C, CONV_WIDTH)
    pooled.append(pool(2))
    cx = cg * proj(OFF_X, CONV_WIDTH)
    pooled.append(pool(3))
    z_ref[0:POOL_HIST, :] = z[ts - POOL_HIST:]
    cx_ref[SUBLANES:, :] = cx
    cx_ext = cx_ref[...]
    cw = cw_ref[...]
    conv = (cw[2:3] * cx + cw[1:2] * _shift_rows(cx_ext, 1, SUBLANES)
            + cw[0:1] * _shift_rows(cx_ext, 2, SUBLANES))
    yb = (bg * conv).astype(BF16)
    cx_ref[0:SUBLANES, :] = cx[ts - SUBLANES:]

    u = _gelu(proj(OFF_U, SGU_WIDTH))
    row = lax.broadcasted_iota(jnp.int32, (SGU_BLOCK, 2 * SGU_BLOCK), 0)
    col = lax.broadcasted_iota(jnp.int32, (SGU_BLOCK, 2 * SGU_BLOCK), 1)
    visible = ((col % SGU_BLOCK) // CHUNK) <= (row // CHUNK)
    zero = jnp.zeros((), BF16)
    blocks = [slice(n * SGU_BLOCK, (n + 1) * SGU_BLOCK) for n in range(ts // SGU_BLOCK)]
    for k, cols in enumerate(slabs):
        mean = head_mean(v_sums[k][0])
        var = head_mean(v_sums[k][1]) - mean * mean
        vn = ((v[:, cols] - mean) * lax.rsqrt(var + EPS) * ln_ref[:, cols]).astype(BF16)
        top = jnp.where(first_head, vn, zero)
        bot = jnp.where(first_head, zero, vn)
        rhs = jnp.concatenate(
            [jnp.concatenate([top[rows], bot[rows]], axis=0) for rows in blocks], axis=1)
        w_pair = jnp.where(visible, sw_ref[k], zero)
        r = _dot(w_pair, rhs)
        bias = sb_ref[:, cols]
        for n, rows in enumerate(blocks):
            sp = r[:, n * LANES:(n + 1) * LANES] + bias
            ya_ref[rows, cols] = (u[rows, cols] * sp).astype(BF16)

    yc = jnp.concatenate(
        [_dot(jnp.concatenate(pooled[0:2], axis=1).astype(BF16), pw_ref[0]),
         _dot(jnp.concatenate(pooled[2:4], axis=1).astype(BF16), pw_ref[1])], axis=1)
    yc = (yc * ps_ref[...]).astype(BF16)

    def gate(i):
        return jax.nn.sigmoid(proj(OFF_G + i * D_MODEL, D_MODEL))

    merged = gate(0) * _dot(ya_ref[...], wa_ref[...])
    merged = merged + gate(1) * _dot(yb, wb_ref[...])
    merged = merged + gate(2) * _dot(yc, wc_ref[...])
    return x + _dot(merged.astype(BF16), wo_ref[...])


def _ffn_kernel(x_ref, nf_ref, wup_ref, cw_ref, wdn_ref, nfin_ref, o_ref,
                act_ref, carry_ref, shift_ref, *, final_norm):
    @pl.when(pl.program_id(1) == 0)
    def _():
        carry_ref[...] = jnp.zeros(carry_ref.shape, F32)

    tiles = _tiles(x_ref)
    hs = [_rmsnorm(x_ref[rows, :], nf_ref[...]).astype(BF16) for rows in tiles]
    for s, rows in enumerate(tiles):
        y = _ffn_tile(x_ref[rows, :], hs[s], wup_ref, cw_ref, wdn_ref, act_ref.at[s],
                      carry_ref, shift_ref.at[s])
        if final_norm:
            y = _rmsnorm(y, nfin_ref[...])
        o_ref[rows, :] = y


def _ffn_tile(x, h, wup_ref, cw_ref, wdn_ref, act_ref, carry_ref, shift_ref):
    ts = x.shape[0]

    def conv_cols(lo, buf):
        cols = slice(lo, lo + FFN_COLS)
        pre = _dot(h, wup_ref[:, cols])
        buf[0:SUBLANES, :] = carry_ref[:, cols]
        buf[SUBLANES:, :] = pre
        cw = cw_ref[:, cols]
        carry_ref[:, cols] = pre[ts - SUBLANES:]
        return (cw[2:3] * pre + cw[1:2] * buf[pl.ds(SUBLANES - 1, ts), :]
                + cw[0:1] * buf[pl.ds(SUBLANES - 2, ts), :])

    for c in range(D_FF // FFN_COLS):
        ag = conv_cols(c * FFN_COLS, shift_ref.at[(2 * c) % SHIFT_BUFS])
        av = conv_cols(D_FF + c * FFN_COLS, shift_ref.at[(2 * c + 1) % SHIFT_BUFS])
        act_ref[:, c * FFN_COLS:(c + 1) * FFN_COLS] = (ag * jax.nn.sigmoid(ag) * av).astype(BF16)

    return x + _dot(act_ref[...], wdn_ref[...])


def _token_spec(rows):
    return pl.BlockSpec((None, rows, D_MODEL), lambda b, t: (b, t, 0))


def _compiler_params():
    return pltpu.CompilerParams(dimension_semantics=("arbitrary", "arbitrary"),
                                vmem_limit_bytes=VMEM_LIMIT)


def _with_casts(body, n_consts, n_casts, *refs):
    x_ref, refs = refs[0], refs[1:]
    consts, refs = refs[:n_consts], refs[n_consts:]
    cast_in, refs = refs[:n_casts], refs[n_casts:]
    o_ref, refs = refs[0], refs[1:]
    cast_out, scratch = refs[:n_casts], refs[n_casts:]
    body(x_ref, *consts, o_ref, *scratch)
    for src, dst in zip(cast_in, cast_out):
        dst[...] = src[...].astype(BF16)


def _layer_call(body, name, x, consts, casts, scratch_shapes):
    bn, s, _ = x.shape
    rows_per_step = SEQ_TILE * SUBTILES
    nt = s // rows_per_step
    steps = bn * nt

    in_specs = [_token_spec(rows_per_step)]
    for a, l in consts:
        if l is None:
            in_specs.append(pl.BlockSpec(a.shape, lambda b, t, n=a.ndim: (0,) * n,
                                         pipeline_mode=pl.Buffered(1)))
        else:
            in_specs.append(pl.BlockSpec((None,) + a.shape[1:],
                                         lambda b, t, l=l, n=a.ndim - 1: (l,) + (0,) * n,
                                         pipeline_mode=pl.Buffered(1)))
    out_specs = [_token_spec(rows_per_step)]
    out_shape = [jax.ShapeDtypeStruct(x.shape, F32)]
    for w, l in casts:
        _, rows, cols = w.shape
        assert rows % steps == 0
        chunk = rows // steps
        in_specs.append(pl.BlockSpec((None, chunk, cols), lambda b, t, l=l: (l, b * nt + t, 0)))
        out_specs.append(pl.BlockSpec((chunk, cols), lambda b, t: (b * nt + t, 0)))
        out_shape.append(jax.ShapeDtypeStruct((rows, cols), BF16))

    return pl.pallas_call(
        functools.partial(_with_casts, body, len(consts), len(casts)),
        grid=(bn, nt),
        in_specs=in_specs,
        out_specs=out_specs,
        out_shape=out_shape,
        scratch_shapes=scratch_shapes,
        compiler_params=_compiler_params(),
        name=name,
    )(x, *[a for a, _ in consts], *[w for w, _ in casts])


def kernel(x, norm_mix, w_in, sgu_ln, sgu_w, sgu_b, conv_b_w, pool_w, pool_scale,
           w_br_a, w_br_b, w_br_c, w_o, norm_ffn, w_up, ffn_conv_w, w_down, norm_f):
    depth = w_in.shape[0]
    ts = SEQ_TILE
    assert x.shape[1] % (ts * SUBTILES) == 0 and ts % SGU_BLOCK == 0
    assert D_FF % FFN_COLS == 0

    row = lambda a: a[:, None, :]
    sw = (sgu_w.astype(BF16).reshape(depth, SGU_GROUPS // 2, 2, SGU_BLOCK, SGU_BLOCK)
          .transpose(0, 1, 3, 2, 4).reshape(depth, SGU_GROUPS // 2, SGU_BLOCK, 2 * SGU_BLOCK))
    sb = jnp.repeat(sgu_b.transpose(0, 2, 1), SGU_HEAD, axis=2)
    p = pool_w.astype(BF16).reshape(depth, 2, 2, POOL_HEAD, POOL_HEAD)
    eye = jnp.eye(2, dtype=BF16)
    pw = jnp.einsum('lhacd,ab->lhacbd', p, eye).reshape(depth, 2, 2 * POOL_HEAD, 2 * POOL_HEAD)

    mixer_scratch = [pltpu.VMEM((SUBTILES, ts, SGU_WIDTH), BF16),
                     pltpu.VMEM((SUBLANES + ts, CONV_WIDTH), F32),
                     pltpu.VMEM((POOL_HIST + ts, POOL_WIDTH), F32)]
    ffn_scratch = [pltpu.VMEM((SUBTILES, ts, D_FF), BF16),
                   pltpu.VMEM((SUBLANES, 2 * D_FF), F32),
                   pltpu.VMEM((SUBTILES, SHIFT_BUFS, SUBLANES + ts, FFN_COLS), F32)]
    mixer_weights = (w_in, w_br_a, w_br_b, w_br_c, w_o)

    wi, wa, wb, wc, wo = [w[0].astype(BF16) for w in mixer_weights]
    for l in range(depth):
        last = l == depth - 1
        x, wup, wdn = _layer_call(
            _mixer_kernel, "mixer", x,
            [(row(norm_mix), l), (wi, None), (row(sgu_ln), l), (sw, l), (sb, l), (conv_b_w, l),
             (pw, l), (row(pool_scale), l), (wa, None), (wb, None), (wc, None), (wo, None)],
            [(w_up, l), (w_down, l)], mixer_scratch)
        x, *next_weights = _layer_call(
            functools.partial(_ffn_kernel, final_norm=last), "ffn", x,
            [(row(norm_ffn), l), (wup, None), (ffn_conv_w, l), (wdn, None), (norm_f[None], None)],
            [] if last else [(w, l + 1) for w in mixer_weights], ffn_scratch)
        if not last:
            wi, wa, wb, wc, wo = next_weights
    return x
```

```python
import functools

import jax
import jax.numpy as jnp
from jax import lax
from jax.experimental import pallas as pl
from jax.experimental.pallas import tpu as pltpu

D_MODEL = 1024
EPS = 1e-6
CHUNK = 64
SGU_WIDTH = 512
SGU_GROUPS = 8
SGU_HEAD = SGU_WIDTH // SGU_GROUPS
SGU_BLOCK = 128
CONV_WIDTH = 512
POOL_WIDTH = 512
POOL_WINDOWS = (2, 4, 8, 16)
POOL_HEAD = POOL_WIDTH // len(POOL_WINDOWS)
D_FF = 2816

OFF_U = 0
OFF_V = OFF_U + SGU_WIDTH
OFF_B = OFF_V + SGU_WIDTH
OFF_C = OFF_B + CONV_WIDTH
OFF_X = OFF_C + CONV_WIDTH
OFF_Z = OFF_X + CONV_WIDTH
OFF_G = OFF_Z + POOL_WIDTH

LANES = 128
SUBLANES = 8
MXU_DIM = 256
SEQ_TILE = 1024
SUBTILES = 1
FFN_COLS = MXU_DIM
SHIFT_BUFS = 4
POOL_HIST = 16
VMEM_LIMIT = 60 * 1024 * 1024

F32 = jnp.float32
BF16 = jnp.bfloat16


def _dot(a, b):
    return jnp.dot(a, b, preferred_element_type=F32)


def _rmsnorm(x, g):
    return x * lax.rsqrt(jnp.mean(x * x, axis=-1, keepdims=True) + EPS) * g


def _gelu(x):
    return 0.5 * x * (1.0 + lax.erf(x * (0.5 ** 0.5)))


def _tiles(x_ref):
    ts = SEQ_TILE
    return [slice(s * ts, (s + 1) * ts) for s in range(x_ref.shape[0] // ts)]


def _mixer_kernel(x_ref, nm_ref, win_ref, ln_ref, sw_ref, sb_ref, cw_ref,
                  pw_ref, ps_ref, wa_ref, wb_ref, wc_ref, wo_ref, o_ref,
                  ya_ref, cx_ref, z_ref):
    @pl.when(pl.program_id(1) == 0)
    def _():
        cx_ref[0:SUBLANES, :] = jnp.zeros((SUBLANES, CONV_WIDTH), F32)
        z_ref[0:POOL_HIST, :] = jnp.zeros((POOL_HIST, POOL_WIDTH), F32)

    tiles = _tiles(x_ref)
    hs = [_rmsnorm(x_ref[rows, :], nm_ref[...]).astype(BF16) for rows in tiles]
    for s, rows in enumerate(tiles):
        tile_index = pl.program_id(1) * len(tiles) + s
        o_ref[rows, :] = _mixer_tile(
            x_ref[rows, :], hs[s], tile_index, win_ref, ln_ref, sw_ref, sb_ref, cw_ref, pw_ref,
            ps_ref, wa_ref, wb_ref, wc_ref, wo_ref, ya_ref.at[s], cx_ref, z_ref)


def _mixer_tile(x, h, tile_index, win_ref, ln_ref, sw_ref, sb_ref, cw_ref, pw_ref, ps_ref,
                wa_ref, wb_ref, wc_ref, wo_ref, ya_ref, cx_ref, z_ref):
    ts = x.shape[0]

    def proj(lo, width):
        return _dot(h, win_ref[:, lo:lo + width])

    first_head = lax.broadcasted_iota(jnp.int32, (1, LANES), 1) < SGU_HEAD
    sign = jnp.where(first_head, 1.0, -1.0)
    slabs = [slice(k * LANES, (k + 1) * LANES) for k in range(SGU_WIDTH // LANES)]

    def head_sums(a):
        return (jnp.sum(a, axis=-1, keepdims=True), jnp.sum(a * sign, axis=-1, keepdims=True))

    def head_mean(sums):
        return (sums[0] + sign * sums[1]) * (0.5 / SGU_HEAD)

    v = _gelu(proj(OFF_V, SGU_WIDTH))
    v_sums = [(head_sums(v[:, c]), head_sums(v[:, c] * v[:, c])) for c in slabs]

    z = proj(OFF_Z, POOL_WIDTH)
    z_ref[POOL_HIST:, :] = z
    pos = lax.broadcasted_iota(jnp.int32, (ts, POOL_HEAD), 0) + tile_index * ts + 1

    def pool(g):
        win = POOL_WINDOWS[g]
        cols = slice(g * POOL_HEAD, (g + 1) * POOL_HEAD)
        s = z_ref[:, cols]
        span = 1
        while span < win:
            s = s + pltpu.roll(s, span, 0)
            span *= 2
        count = jnp.minimum(pos, win).astype(F32)
        return s[POOL_HIST:] / count - z[:, cols]

    bg = proj(OFF_B, CONV_WIDTH)
    pooled = [pool(0), pool(1)]
    cg = proj(OFF_C, CONV_WIDTH)
    pooled.append(pool(2))
    cx = cg * proj(OFF_X, CONV_WIDTH)
    pooled.append(pool(3))
    z_ref[0:POOL_HIST, :] = z[ts - POOL_HIST:]
    cx_ref[SUBLANES:, :] = cx
    cw = cw_ref[...]
    conv = (cw[2:3] * cx + cw[1:2] * cx_ref[pl.ds(SUBLANES - 1, ts), :]
            + cw[0:1] * cx_ref[pl.ds(SUBLANES - 2, ts), :])
    yb = (bg * conv).astype(BF16)
    cx_ref[0:SUBLANES, :] = cx[ts - SUBLANES:]

    u = _gelu(proj(OFF_U, SGU_WIDTH))
    row = lax.broadcasted_iota(jnp.int32, (SGU_BLOCK, 2 * SGU_BLOCK), 0)
    col = lax.broadcasted_iota(jnp.int32, (SGU_BLOCK, 2 * SGU_BLOCK), 1)
    visible = ((col % SGU_BLOCK) // CHUNK) <= (row // CHUNK)
    zero = jnp.zeros((), BF16)
    blocks = [slice(n * SGU_BLOCK, (n + 1) * SGU_BLOCK) for n in range(ts // SGU_BLOCK)]
    for k, cols in enumerate(slabs):
        mean = head_mean(v_sums[k][0])
        var = head_mean(v_sums[k][1]) - mean * mean
        vn = ((v[:, cols] - mean) * lax.rsqrt(var + EPS) * ln_ref[:, cols]).astype(BF16)
        top = jnp.where(first_head, vn, zero)
        bot = jnp.where(first_head, zero, vn)
        rhs = jnp.concatenate(
            [jnp.concatenate([top[rows], bot[rows]], axis=0) for rows in blocks], axis=1)
        w_pair = jnp.where(visible, sw_ref[k], zero)
        r = _dot(w_pair, rhs)
        bias = sb_ref[:, cols]
        for n, rows in enumerate(blocks):
            sp = r[:, n * LANES:(n + 1) * LANES] + bias
            ya_ref[rows, cols] = (u[rows, cols] * sp).astype(BF16)

    yc = jnp.concatenate(
        [_dot(jnp.concatenate(pooled[0:2], axis=1).astype(BF16), pw_ref[0]),
         _dot(jnp.concatenate(pooled[2:4], axis=1).astype(BF16), pw_ref[1])], axis=1)
    yc = (yc * ps_ref[...]).astype(BF16)

    def gate(i):
        return jax.nn.sigmoid(proj(OFF_G + i * D_MODEL, D_MODEL))

    merged = gate(0) * _dot(ya_ref[...], wa_ref[...])
    merged = merged + gate(1) * _dot(yb, wb_ref[...])
    merged = merged + gate(2) * _dot(yc, wc_ref[...])
    return x + _dot(merged.astype(BF16), wo_ref[...])


def _ffn_kernel(x_ref, nf_ref, wup_ref, cw_ref, wdn_ref, nfin_ref, o_ref,
                act_ref, carry_ref, shift_ref, *, final_norm):
    @pl.when(pl.program_id(1) == 0)
    def _():
        carry_ref[...] = jnp.zeros(carry_ref.shape, F32)

    tiles = _tiles(x_ref)
    hs = [_rmsnorm(x_ref[rows, :], nf_ref[...]).astype(BF16) for rows in tiles]
    for s, rows in enumerate(tiles):
        y = _ffn_tile(x_ref[rows, :], hs[s], wup_ref, cw_ref, wdn_ref, act_ref.at[s],
                      carry_ref, shift_ref.at[s])
        if final_norm:
            y = _rmsnorm(y, nfin_ref[...])
        o_ref[rows, :] = y


def _ffn_tile(x, h, wup_ref, cw_ref, wdn_ref, act_ref, carry_ref, shift_ref):
    ts = x.shape[0]

    def conv_cols(lo, buf):
        cols = slice(lo, lo + FFN_COLS)
        pre = _dot(h, wup_ref[:, cols])
        buf[0:SUBLANES, :] = carry_ref[:, cols]
        buf[SUBLANES:, :] = pre
        cw = cw_ref[:, cols]
        carry_ref[:, cols] = pre[ts - SUBLANES:]
        return (cw[2:3] * pre + cw[1:2] * buf[pl.ds(SUBLANES - 1, ts), :]
                + cw[0:1] * buf[pl.ds(SUBLANES - 2, ts), :])

    for c in range(D_FF // FFN_COLS):
        ag = conv_cols(c * FFN_COLS, shift_ref.at[(2 * c) % SHIFT_BUFS])
        av = conv_cols(D_FF + c * FFN_COLS, shift_ref.at[(2 * c + 1) % SHIFT_BUFS])
        act_ref[:, c * FFN_COLS:(c + 1) * FFN_COLS] = (ag * jax.nn.sigmoid(ag) * av).astype(BF16)

    return x + _dot(act_ref[...], wdn_ref[...])


def _token_spec(rows):
    return pl.BlockSpec((None, rows, D_MODEL), lambda b, t: (b, t, 0))


def _compiler_params():
    return pltpu.CompilerParams(dimension_semantics=("arbitrary", "arbitrary"),
                                vmem_limit_bytes=VMEM_LIMIT)


def _with_casts(body, n_consts, n_casts, *refs):
    x_ref, refs = refs[0], refs[1:]
    consts, refs = refs[:n_consts], refs[n_consts:]
    cast_in, refs = refs[:n_casts], refs[n_casts:]
    o_ref, refs = refs[0], refs[1:]
    cast_out, scratch = refs[:n_casts], refs[n_casts:]
    body(x_ref, *consts, o_ref, *scratch)
    for src, dst in zip(cast_in, cast_out):
        dst[...] = src[...].astype(BF16)


def _layer_call(body, name, x, consts, casts, scratch_shapes):
    bn, s, _ = x.shape
    rows_per_step = SEQ_TILE * SUBTILES
    nt = s // rows_per_step
    steps = bn * nt

    in_specs = [_token_spec(rows_per_step)]
    for a, l in consts:
        if l is None:
            in_specs.append(pl.BlockSpec(a.shape, lambda b, t, n=a.ndim: (0,) * n,
                                         pipeline_mode=pl.Buffered(1)))
        else:
            in_specs.append(pl.BlockSpec((None,) + a.shape[1:],
                                         lambda b, t, l=l, n=a.ndim - 1: (l,) + (0,) * n,
                                         pipeline_mode=pl.Buffered(1)))
    out_specs = [_token_spec(rows_per_step)]
    out_shape = [jax.ShapeDtypeStruct(x.shape, F32)]
    for w, l in casts:
        _, rows, cols = w.shape
        assert rows % steps == 0
        chunk = rows // steps
        in_specs.append(pl.BlockSpec((None, chunk, cols), lambda b, t, l=l: (l, b * nt + t, 0)))
        out_specs.append(pl.BlockSpec((chunk, cols), lambda b, t: (b * nt + t, 0)))
        out_shape.append(jax.ShapeDtypeStruct((rows, cols), BF16))

    return pl.pallas_call(
        functools.partial(_with_casts, body, len(consts), len(casts)),
        grid=(bn, nt),
        in_specs=in_specs,
        out_specs=out_specs,
        out_shape=out_shape,
        scratch_shapes=scratch_shapes,
        compiler_params=_compiler_params(),
        name=name,
    )(x, *[a for a, _ in consts], *[w for w, _ in casts])


def kernel(x, norm_mix, w_in, sgu_ln, sgu_w, sgu_b, conv_b_w, pool_w, pool_scale,
           w_br_a, w_br_b, w_br_c, w_o, norm_ffn, w_up, ffn_conv_w, w_down, norm_f):
    depth = w_in.shape[0]
    ts = SEQ_TILE
    assert x.shape[1] % (ts * SUBTILES) == 0 and ts % SGU_BLOCK == 0
    assert D_FF % FFN_COLS == 0

    row = lambda a: a[:, None, :]
    sw = (sgu_w.astype(BF16).reshape(depth, SGU_GROUPS // 2, 2, SGU_BLOCK, SGU_BLOCK)
          .transpose(0, 1, 3, 2, 4).reshape(depth, SGU_GROUPS // 2, SGU_BLOCK, 2 * SGU_BLOCK))
    sb = jnp.repeat(sgu_b.transpose(0, 2, 1), SGU_HEAD, axis=2)
    p = pool_w.astype(BF16).reshape(depth, 2, 2, POOL_HEAD, POOL_HEAD)
    eye = jnp.eye(2, dtype=BF16)
    pw = jnp.einsum('lhacd,ab->lhacbd', p, eye).reshape(depth, 2, 2 * POOL_HEAD, 2 * POOL_HEAD)

    mixer_scratch = [pltpu.VMEM((SUBTILES, ts, SGU_WIDTH), BF16),
                     pltpu.VMEM((SUBLANES + ts, CONV_WIDTH), F32),
                     pltpu.VMEM((POOL_HIST + ts, POOL_WIDTH), F32)]
    ffn_scratch = [pltpu.VMEM((SUBTILES, ts, D_FF), BF16),
                   pltpu.VMEM((SUBLANES, 2 * D_FF), F32),
                   pltpu.VMEM((SUBTILES, SHIFT_BUFS, SUBLANES + ts, FFN_COLS), F32)]
    mixer_weights = (w_in, w_br_a, w_br_b, w_br_c, w_o)

    wi, wa, wb, wc, wo = [w[0].astype(BF16) for w in mixer_weights]
    for l in range(depth):
        last = l == depth - 1
        x, wup, wdn = _layer_call(
            _mixer_kernel, "mixer", x,
            [(row(norm_mix), l), (wi, None), (row(sgu_ln), l), (sw, l), (sb, l), (conv_b_w, l),
             (pw, l), (row(pool_scale), l), (wa, None), (wb, None), (wc, None), (wo, None)],
            [(w_up, l), (w_down, l)], mixer_scratch)
        x, *next_weights = _layer_call(
            functools.partial(_ffn_kernel, final_norm=last), "ffn", x,
            [(row(norm_ffn), l), (wup, None), (ffn_conv_w, l), (wdn, None), (norm_f[None], None)],
            [] if last else [(w, l + 1) for w in mixer_weights], ffn_scratch)
        if not last:
            wi, wa, wb, wc, wo = next_weights
    return x
```

```python
import functools

import jax
import jax.numpy as jnp
from jax import lax
from jax.experimental import pallas as pl
from jax.experimental.pallas import tpu as pltpu

D_MODEL = 1024
EPS = 1e-6
CHUNK = 64
SGU_WIDTH = 512
SGU_GROUPS = 8
SGU_HEAD = SGU_WIDTH // SGU_GROUPS
SGU_BLOCK = 128
CONV_WIDTH = 512
POOL_WIDTH = 512
POOL_WINDOWS = (2, 4, 8, 16)
POOL_HEAD = POOL_WIDTH // len(POOL_WINDOWS)
D_FF = 2816

OFF_U = 0
OFF_V = OFF_U + SGU_WIDTH
OFF_B = OFF_V + SGU_WIDTH
OFF_C = OFF_B + CONV_WIDTH
OFF_X = OFF_C + CONV_WIDTH
OFF_Z = OFF_X + CONV_WIDTH
OFF_G = OFF_Z + POOL_WIDTH

LANES = 128
SUBLANES = 8
MXU_DIM = 256
SEQ_TILE = 1024
SUBTILES = 1
FFN_COLS = MXU_DIM
SHIFT_BUFS = 4
POOL_HIST = 16
VMEM_LIMIT = 60 * 1024 * 1024

F32 = jnp.float32
BF16 = jnp.bfloat16


def _dot(a, b):
    return jnp.dot(a, b, preferred_element_type=F32)


def _rmsnorm(x, g):
    return x * lax.rsqrt(jnp.mean(x * x, axis=-1, keepdims=True) + EPS) * g


def _gelu(x):
    return 0.5 * x * (1.0 + lax.erf(x * (0.5 ** 0.5)))


def _tiles(x_ref):
    ts = SEQ_TILE
    return [slice(s * ts, (s + 1) * ts) for s in range(x_ref.shape[0] // ts)]


def _mixer_kernel(x_ref, nm_ref, win_ref, ln_ref, sw_ref, sb_ref, cw_ref,
                  pw_ref, ps_ref, wa_ref, wb_ref, wc_ref, wo_ref, o_ref,
                  ya_ref, cx_ref, z_ref):
    @pl.when(pl.program_id(1) == 0)
    def _():
        cx_ref[0:SUBLANES, :] = jnp.zeros((SUBLANES, CONV_WIDTH), F32)
        z_ref[0:POOL_HIST, :] = jnp.zeros((POOL_HIST, POOL_WIDTH), F32)

    tiles = _tiles(x_ref)
    hs = [_rmsnorm(x_ref[rows, :], nm_ref[...]).astype(BF16) for rows in tiles]
    for s, rows in enumerate(tiles):
        tile_index = pl.program_id(1) * len(tiles) + s
        o_ref[rows, :] = _mixer_tile(
            x_ref[rows, :], hs[s], tile_index, win_ref, ln_ref, sw_ref, sb_ref, cw_ref, pw_ref,
            ps_ref, wa_ref, wb_ref, wc_ref, wo_ref, ya_ref.at[s], cx_ref, z_ref)


def _mixer_tile(x, h, tile_index, win_ref, ln_ref, sw_ref, sb_ref, cw_ref, pw_ref, ps_ref,
                wa_ref, wb_ref, wc_ref, wo_ref, ya_ref, cx_ref, z_ref):
    ts = x.shape[0]

    def proj(lo, width):
        return _dot(h, win_ref[:, lo:lo + width])

    first_head = lax.broadcasted_iota(jnp.int32, (1, LANES), 1) < SGU_HEAD
    sign = jnp.where(first_head, 1.0, -1.0)
    slabs = [slice(k * LANES, (k + 1) * LANES) for k in range(SGU_WIDTH // LANES)]

    def head_sums(a):
        return (jnp.sum(a, axis=-1, keepdims=True), jnp.sum(a * sign, axis=-1, keepdims=True))

    def head_mean(sums):
        return (sums[0] + sign * sums[1]) * (0.5 / SGU_HEAD)

    v = _gelu(proj(OFF_V, SGU_WIDTH))
    v_sums = [(head_sums(v[:, c]), head_sums(v[:, c] * v[:, c])) for c in slabs]

    z = proj(OFF_Z, POOL_WIDTH)
    z_ref[POOL_HIST:, :] = z
    pos = lax.broadcasted_iota(jnp.int32, (ts, POOL_HEAD), 0) + tile_index * ts + 1

    def pool(g):
        win = POOL_WINDOWS[g]
        cols = slice(g * POOL_HEAD, (g + 1) * POOL_HEAD)
        s = z_ref[:, cols]
        span = 1
        while span < win:
            s = s + pltpu.roll(s, span, 0)
            span *= 2
        count = jnp.minimum(pos, win).astype(F32)
        return s[POOL_HIST:] / count - z[:, cols]

    bg = proj(OFF_B, CONV_WIDTH)
    pooled = [pool(0), pool(1)]
    cg = proj(OFF_C, CONV_WIDTH)
    pooled.append(pool(2))
    cx = cg * proj(OFF_X, CONV_WIDTH)
    pooled.append(pool(3))
    z_ref[0:POOL_HIST, :] = z[ts - POOL_HIST:]
    cx_ref[SUBLANES:, :] = cx
    cw = cw_ref[...]
    conv = (cw[2:3] * cx + cw[1:2] * cx_ref[pl.ds(SUBLANES - 1, ts), :]
            + cw[0:1] * cx_ref[pl.ds(SUBLANES - 2, ts), :])
    yb = (bg * conv).astype(BF16)
    cx_ref[0:SUBLANES, :] = cx[ts - SUBLANES:]

    u = _gelu(proj(OFF_U, SGU_WIDTH))
    row = lax.broadcasted_iota(jnp.int32, (SGU_BLOCK, 2 * SGU_BLOCK), 0)
    col = lax.broadcasted_iota(jnp.int32, (SGU_BLOCK, 2 * SGU_BLOCK), 1)
    visible = ((col % SGU_BLOCK) // CHUNK) <= (row // CHUNK)
    zero = jnp.zeros((), BF16)
    blocks = [slice(n * SGU_BLOCK, (n + 1) * SGU_BLOCK) for n in range(ts // SGU_BLOCK)]
    for k, cols in enumerate(slabs):
        mean = head_mean(v_sums[k][0])
        var = head_mean(v_sums[k][1]) - mean * mean
        vn = ((v[:, cols] - mean) * lax.rsqrt(var + EPS) * ln_ref[:, cols]).astype(BF16)
        top = jnp.where(first_head, vn, zero)
        bot = jnp.where(first_head, zero, vn)
        rhs = jnp.concatenate(
            [jnp.concatenate([top[rows], bot[rows]], axis=0) for rows in blocks], axis=1)
        w_pair = jnp.concatenate([sw_ref[2 * k], sw_ref[2 * k + 1]], axis=1).astype(BF16)
        w_pair = jnp.where(visible, w_pair, zero)
        r = _dot(w_pair, rhs)
        bias = sb_ref[:, cols]
        for n, rows in enumerate(blocks):
            sp = r[:, n * LANES:(n + 1) * LANES] + bias
            ya_ref[rows, cols] = (u[rows, cols] * sp).astype(BF16)

    def window_pair_map(g):
        none = jnp.zeros((POOL_HEAD, POOL_HEAD), F32)
        w = jnp.concatenate([jnp.concatenate([pw_ref[g], none], axis=1),
                             jnp.concatenate([none, pw_ref[g + 1]], axis=1)], axis=0)
        return _dot(jnp.concatenate(pooled[g:g + 2], axis=1).astype(BF16), w.astype(BF16))

    yc = jnp.concatenate([window_pair_map(0), window_pair_map(2)], axis=1)
    yc = (yc * ps_ref[...]).astype(BF16)

    def gate(i):
        return jax.nn.sigmoid(proj(OFF_G + i * D_MODEL, D_MODEL))

    merged = gate(0) * _dot(ya_ref[...], wa_ref[...])
    merged = merged + gate(1) * _dot(yb, wb_ref[...])
    merged = merged + gate(2) * _dot(yc, wc_ref[...])
    return x + _dot(merged.astype(BF16), wo_ref[...])


def _ffn_kernel(x_ref, nf_ref, wup_ref, cw_ref, wdn_ref, nfin_ref, o_ref,
                act_ref, carry_ref, shift_ref, *, final_norm):
    @pl.when(pl.program_id(1) == 0)
    def _():
        carry_ref[...] = jnp.zeros(carry_ref.shape, F32)

    tiles = _tiles(x_ref)
    hs = [_rmsnorm(x_ref[rows, :], nf_ref[...]).astype(BF16) for rows in tiles]
    for s, rows in enumerate(tiles):
        y = _ffn_tile(x_ref[rows, :], hs[s], wup_ref, cw_ref, wdn_ref, act_ref.at[s],
                      carry_ref, shift_ref.at[s])
        if final_norm:
            y = _rmsnorm(y, nfin_ref[...])
        o_ref[rows, :] = y


def _ffn_tile(x, h, wup_ref, cw_ref, wdn_ref, act_ref, carry_ref, shift_ref):
    ts = x.shape[0]

    def conv_cols(lo, buf):
        cols = slice(lo, lo + FFN_COLS)
        pre = _dot(h, wup_ref[:, cols])
        buf[0:SUBLANES, :] = carry_ref[:, cols]
        buf[SUBLANES:, :] = pre
        cw = cw_ref[:, cols]
        carry_ref[:, cols] = pre[ts - SUBLANES:]
        return (cw[2:3] * pre + cw[1:2] * buf[pl.ds(SUBLANES - 1, ts), :]
                + cw[0:1] * buf[pl.ds(SUBLANES - 2, ts), :])

    for c in range(D_FF // FFN_COLS):
        ag = conv_cols(c * FFN_COLS, shift_ref.at[(2 * c) % SHIFT_BUFS])
        av = conv_cols(D_FF + c * FFN_COLS, shift_ref.at[(2 * c + 1) % SHIFT_BUFS])
        act_ref[:, c * FFN_COLS:(c + 1) * FFN_COLS] = (ag * jax.nn.sigmoid(ag) * av).astype(BF16)

    return x + _dot(act_ref[...], wdn_ref[...])


def _token_spec(rows):
    return pl.BlockSpec((None, rows, D_MODEL), lambda b, t: (b, t, 0))


def _compiler_params():
    return pltpu.CompilerParams(dimension_semantics=("arbitrary", "arbitrary"),
                                vmem_limit_bytes=VMEM_LIMIT)


def _with_casts(body, table_rows, n_casts, *refs):
    x_ref, refs = refs[0], refs[1:]
    n_consts = len(table_rows)
    consts, refs = refs[:n_consts], refs[n_consts:]
    consts = [c if r is None else c.at[pl.ds(r, 1), :] for c, r in zip(consts, table_rows)]
    cast_in, refs = refs[:n_casts], refs[n_casts:]
    o_ref, refs = refs[0], refs[1:]
    cast_out, scratch = refs[:n_casts], refs[n_casts:]
    body(x_ref, *consts, o_ref, *scratch)
    for src, dst in zip(cast_in, cast_out):
        dst[...] = src[...].astype(BF16)


def _layer_call(body, name, x, consts, casts, scratch_shapes):
    bn, s, _ = x.shape
    rows_per_step = SEQ_TILE * SUBTILES
    nt = s // rows_per_step
    steps = bn * nt

    in_specs = [_token_spec(rows_per_step)]
    for a, l in consts:
        if l is None or a.ndim == 2:
            in_specs.append(pl.BlockSpec(a.shape, lambda b, t, n=a.ndim: (0,) * n,
                                         pipeline_mode=pl.Buffered(1)))
        else:
            in_specs.append(pl.BlockSpec((None,) + a.shape[1:],
                                         lambda b, t, l=l, n=a.ndim - 1: (l,) + (0,) * n,
                                         pipeline_mode=pl.Buffered(1)))
    out_specs = [_token_spec(rows_per_step)]
    out_shape = [jax.ShapeDtypeStruct(x.shape, F32)]
    for w, l in casts:
        _, rows, cols = w.shape
        assert rows % steps == 0
        chunk = rows // steps
        in_specs.append(pl.BlockSpec((None, chunk, cols), lambda b, t, l=l: (l, b * nt + t, 0)))
        out_specs.append(pl.BlockSpec((chunk, cols), lambda b, t: (b * nt + t, 0)))
        out_shape.append(jax.ShapeDtypeStruct((rows, cols), BF16))

    return pl.pallas_call(
        functools.partial(_with_casts, body,
                          tuple(l if a.ndim == 2 else None for a, l in consts), len(casts)),
        grid=(bn, nt),
        in_specs=in_specs,
        out_specs=out_specs,
        out_shape=out_shape,
        scratch_shapes=scratch_shapes,
        compiler_params=_compiler_params(),
        name=name,
    )(x, *[a for a, _ in consts], *[w for w, _ in casts])


def kernel(x, norm_mix, w_in, sgu_ln, sgu_w, sgu_b, conv_b_w, pool_w, pool_scale,
           w_br_a, w_br_b, w_br_c, w_o, norm_ffn, w_up, ffn_conv_w, w_down, norm_f):
    depth = w_in.shape[0]
    ts = SEQ_TILE
    assert x.shape[1] % (ts * SUBTILES) == 0 and ts % SGU_BLOCK == 0
    assert D_FF % FFN_COLS == 0

    sb = jnp.repeat(sgu_b.transpose(0, 2, 1), SGU_HEAD, axis=2)

    mixer_scratch = [pltpu.VMEM((SUBTILES, ts, SGU_WIDTH), BF16),
                     pltpu.VMEM((SUBLANES + ts, CONV_WIDTH), F32),
                     pltpu.VMEM((POOL_HIST + ts, POOL_WIDTH), F32)]
    ffn_scratch = [pltpu.VMEM((SUBTILES, ts, D_FF), BF16),
                   pltpu.VMEM((SUBLANES, 2 * D_FF), F32),
                   pltpu.VMEM((SUBTILES, SHIFT_BUFS, SUBLANES + ts, FFN_COLS), F32)]
    mixer_weights = (w_in, w_br_a, w_br_b, w_br_c, w_o)

    wi, wa, wb, wc, wo = [w[0].astype(BF16) for w in mixer_weights]
    for l in range(depth):
        last = l == depth - 1
        x, wup, wdn = _layer_call(
            _mixer_kernel, "mixer", x,
            [(norm_mix, l), (wi, None), (sgu_ln, l), (sgu_w, l), (sb, l), (conv_b_w, l),
             (pool_w, l), (pool_scale, l), (wa, None), (wb, None), (wc, None), (wo, None)],
            [(w_up, l), (w_down, l)], mixer_scratch)
        x, *next_weights = _layer_call(
            functools.partial(_ffn_kernel, final_norm=last), "ffn", x,
            [(norm_ffn, l), (wup, None), (ffn_conv_w, l), (wdn, None), (norm_f[None], None)],
            [] if last else [(w, l + 1) for w in mixer_weights], ffn_scratch)
        if not last:
            wi, wa, wb, wc, wo = next_weights
    return x
```

```python
import functools

import jax
import jax.numpy as jnp
from jax import lax
from jax.experimental import pallas as pl
from jax.experimental.pallas import tpu as pltpu

D_MODEL = 1024
EPS = 1e-6
CHUNK = 64
SGU_WIDTH = 512
SGU_GROUPS = 8
SGU_HEAD = SGU_WIDTH // SGU_GROUPS
SGU_BLOCK = 128
CONV_WIDTH = 512
POOL_WIDTH = 512
POOL_WINDOWS = (2, 4, 8, 16)
POOL_HEAD = POOL_WIDTH // len(POOL_WINDOWS)
D_FF = 2816

OFF_U = 0
OFF_V = OFF_U + SGU_WIDTH
OFF_B = OFF_V + SGU_WIDTH
OFF_C = OFF_B + CONV_WIDTH
OFF_X = OFF_C + CONV_WIDTH
OFF_Z = OFF_X + CONV_WIDTH
OFF_G = OFF_Z + POOL_WIDTH

LANES = 128
SUBLANES = 8
MXU_DIM = 256
SEQ_TILE = 1024
SUBTILES = 1
FFN_COLS = MXU_DIM
SHIFT_BUFS = 4
POOL_HIST = 16
VMEM_LIMIT = 60 * 1024 * 1024

F32 = jnp.float32
BF16 = jnp.bfloat16


def _dot(a, b):
    return jnp.dot(a, b, preferred_element_type=F32)


def _rmsnorm(x, g):
    return x * lax.rsqrt(jnp.mean(x * x, axis=-1, keepdims=True) + EPS) * g


def _rmsnorm_bf16(x, g):
    r = lax.rsqrt(jnp.mean(x * x, axis=-1, keepdims=True) + EPS)
    return x.astype(BF16) * r.astype(BF16) * g.astype(BF16)


def _gelu(x):
    return 0.5 * x * (1.0 + lax.erf(x * (0.5 ** 0.5)))


def _tiles(x_ref):
    ts = SEQ_TILE
    return [slice(s * ts, (s + 1) * ts) for s in range(x_ref.shape[0] // ts)]


def _mixer_kernel(x_ref, nm_ref, win_ref, ln_ref, sw_ref, sb_ref, cw_ref,
                  pw_ref, ps_ref, wa_ref, wb_ref, wc_ref, wo_ref, o_ref,
                  ya_ref, cx_ref, z_ref):
    @pl.when(pl.program_id(1) == 0)
    def _():
        cx_ref[0:SUBLANES, :] = jnp.zeros((SUBLANES, CONV_WIDTH), F32)
        z_ref[0:POOL_HIST, :] = jnp.zeros((POOL_HIST, POOL_WIDTH), F32)

    tiles = _tiles(x_ref)
    hs = [_rmsnorm_bf16(x_ref[rows, :], nm_ref[...]) for rows in tiles]
    for s, rows in enumerate(tiles):
        tile_index = pl.program_id(1) * len(tiles) + s
        o_ref[rows, :] = _mixer_tile(
            x_ref[rows, :], hs[s], tile_index, win_ref, ln_ref, sw_ref, sb_ref, cw_ref, pw_ref,
            ps_ref, wa_ref, wb_ref, wc_ref, wo_ref, ya_ref.at[s], cx_ref, z_ref)


def _mixer_tile(x, h, tile_index, win_ref, ln_ref, sw_ref, sb_ref, cw_ref, pw_ref, ps_ref,
                wa_ref, wb_ref, wc_ref, wo_ref, ya_ref, cx_ref, z_ref):
    ts = x.shape[0]

    def proj(lo, width):
        return _dot(h, win_ref[:, lo:lo + width])

    first_head = lax.broadcasted_iota(jnp.int32, (1, LANES), 1) < SGU_HEAD
    sign = jnp.where(first_head, 1.0, -1.0)
    slabs = [slice(k * LANES, (k + 1) * LANES) for k in range(SGU_WIDTH // LANES)]

    def head_sums(a):
        return (jnp.sum(a, axis=-1, keepdims=True), jnp.sum(a * sign, axis=-1, keepdims=True))

    def head_mean(sums):
        return (sums[0] + sign * sums[1]) * (0.5 / SGU_HEAD)

    v = _gelu(proj(OFF_V, SGU_WIDTH))
    v_sums = [(head_sums(v[:, c]), head_sums(v[:, c] * v[:, c])) for c in slabs]

    z = proj(OFF_Z, POOL_WIDTH)
    z_ref[POOL_HIST:, :] = z
    pos = lax.broadcasted_iota(jnp.int32, (ts, POOL_HEAD), 0) + tile_index * ts + 1

    def pool(g):
        win = POOL_WINDOWS[g]
        cols = slice(g * POOL_HEAD, (g + 1) * POOL_HEAD)
        s = z_ref[:, cols]
        span = 1
        while span < win:
            s = s + pltpu.roll(s, span, 0)
            span *= 2
        count = jnp.minimum(pos, win).astype(F32)
        return s[POOL_HIST:] / count - z[:, cols]

    bg = proj(OFF_B, CONV_WIDTH)
    pooled = [pool(0), pool(1)]
    cg = proj(OFF_C, CONV_WIDTH)
    pooled.append(pool(2))
    cx = cg * proj(OFF_X, CONV_WIDTH)
    pooled.append(pool(3))
    z_ref[0:POOL_HIST, :] = z[ts - POOL_HIST:]
    cx_ref[SUBLANES:, :] = cx
    cw = cw_ref[...]
    conv = (cw[2:3] * cx + cw[1:2] * cx_ref[pl.ds(SUBLANES - 1, ts), :]
            + cw[0:1] * cx_ref[pl.ds(SUBLANES - 2, ts), :])
    yb = (bg * conv).astype(BF16)
    cx_ref[0:SUBLANES, :] = cx[ts - SUBLANES:]

    u = _gelu(proj(OFF_U, SGU_WIDTH))
    row = lax.broadcasted_iota(jnp.int32, (SGU_BLOCK, 2 * SGU_BLOCK), 0)
    col = lax.broadcasted_iota(jnp.int32, (SGU_BLOCK, 2 * SGU_BLOCK), 1)
    visible = ((col % SGU_BLOCK) // CHUNK) <= (row // CHUNK)
    zero = jnp.zeros((), BF16)
    blocks = [slice(n * SGU_BLOCK, (n + 1) * SGU_BLOCK) for n in range(ts // SGU_BLOCK)]
    for k, cols in enumerate(slabs):
        mean = head_mean(v_sums[k][0])
        var = head_mean(v_sums[k][1]) - mean * mean
        vn = ((v[:, cols] - mean) * lax.rsqrt(var + EPS) * ln_ref[:, cols]).astype(BF16)
        top = jnp.where(first_head, vn, zero)
        bot = jnp.where(first_head, zero, vn)
        rhs = jnp.concatenate(
            [jnp.concatenate([top[rows], bot[rows]], axis=0) for rows in blocks], axis=1)
        w_pair = jnp.concatenate([sw_ref[2 * k], sw_ref[2 * k + 1]], axis=1).astype(BF16)
        w_pair = jnp.where(visible, w_pair, zero)
        r = _dot(w_pair, rhs)
        bias = sb_ref[:, cols]
        for n, rows in enumerate(blocks):
            sp = r[:, n * LANES:(n + 1) * LANES] + bias
            ya_ref[rows, cols] = (u[rows, cols] * sp).astype(BF16)

    def window_pair_map(g):
        none = jnp.zeros((POOL_HEAD, POOL_HEAD), F32)
        w = jnp.concatenate([jnp.concatenate([pw_ref[g], none], axis=1),
                             jnp.concatenate([none, pw_ref[g + 1]], axis=1)], axis=0)
        return _dot(jnp.concatenate(pooled[g:g + 2], axis=1).astype(BF16), w.astype(BF16))

    yc = jnp.concatenate([window_pair_map(0), window_pair_map(2)], axis=1)
    yc = (yc * ps_ref[...]).astype(BF16)

    def gate(i):
        return jax.nn.sigmoid(proj(OFF_G + i * D_MODEL, D_MODEL))

    merged = gate(0) * _dot(ya_ref[...], wa_ref[...])
    merged = merged + gate(1) * _dot(yb, wb_ref[...])
    merged = merged + gate(2) * _dot(yc, wc_ref[...])
    return x + _dot(merged.astype(BF16), wo_ref[...])


def _ffn_kernel(x_ref, nf_ref, wup_ref, cw_ref, wdn_ref, nfin_ref, o_ref,
                act_ref, carry_ref, shift_ref, *, final_norm):
    @pl.when(pl.program_id(1) == 0)
    def _():
        carry_ref[...] = jnp.zeros(carry_ref.shape, F32)

    tiles = _tiles(x_ref)
    hs = [_rmsnorm_bf16(x_ref[rows, :], nf_ref[...]) for rows in tiles]
    for s, rows in enumerate(tiles):
        y = _ffn_tile(x_ref[rows, :], hs[s], wup_ref, cw_ref, wdn_ref, act_ref.at[s],
                      carry_ref, shift_ref.at[s])
        if final_norm:
            y = _rmsnorm(y, nfin_ref[...])
        o_ref[rows, :] = y


def _ffn_tile(x, h, wup_ref, cw_ref, wdn_ref, act_ref, carry_ref, shift_ref):
    ts = x.shape[0]

    def conv_cols(lo, buf):
        cols = slice(lo, lo + FFN_COLS)
        pre = _dot(h, wup_ref[:, cols])
        buf[0:SUBLANES, :] = carry_ref[:, cols]
        buf[SUBLANES:, :] = pre
        cw = cw_ref[:, cols]
        carry_ref[:, cols] = pre[ts - SUBLANES:]
        return (cw[2:3] * pre + cw[1:2] * buf[pl.ds(SUBLANES - 1, ts), :]
                + cw[0:1] * buf[pl.ds(SUBLANES - 2, ts), :])

    for c in range(D_FF // FFN_COLS):
        ag = conv_cols(c * FFN_COLS, shift_ref.at[(2 * c) % SHIFT_BUFS])
        av = conv_cols(D_FF + c * FFN_COLS, shift_ref.at[(2 * c + 1) % SHIFT_BUFS])
        act_ref[:, c * FFN_COLS:(c + 1) * FFN_COLS] = (ag * jax.nn.sigmoid(ag) * av).astype(BF16)

    return x + _dot(act_ref[...], wdn_ref[...])


def _token_spec(rows):
    return pl.BlockSpec((None, rows, D_MODEL), lambda b, t: (b, t, 0))


def _compiler_params():
    return pltpu.CompilerParams(dimension_semantics=("arbitrary", "arbitrary"),
                                vmem_limit_bytes=VMEM_LIMIT)


def _with_casts(body, table_rows, n_casts, *refs):
    x_ref, refs = refs[0], refs[1:]
    n_consts = len(table_rows)
    consts, refs = refs[:n_consts], refs[n_consts:]
    consts = [c if r is None else c.at[pl.ds(r, 1), :] for c, r in zip(consts, table_rows)]
    cast_in, refs = refs[:n_casts], refs[n_casts:]
    o_ref, refs = refs[0], refs[1:]
    cast_out, scratch = refs[:n_casts], refs[n_casts:]
    body(x_ref, *consts, o_ref, *scratch)
    for src, dst in zip(cast_in, cast_out):
        dst[...] = src[...].astype(BF16)


def _layer_call(body, name, x, consts, casts, scratch_shapes):
    bn, s, _ = x.shape
    rows_per_step = SEQ_TILE * SUBTILES
    nt = s // rows_per_step
    steps = bn * nt

    in_specs = [_token_spec(rows_per_step)]
    for a, l in consts:
        if l is None or a.ndim == 2:
            in_specs.append(pl.BlockSpec(a.shape, lambda b, t, n=a.ndim: (0,) * n,
                                         pipeline_mode=pl.Buffered(1)))
        else:
            in_specs.append(pl.BlockSpec((None,) + a.shape[1:],
                                         lambda b, t, l=l, n=a.ndim - 1: (l,) + (0,) * n,
                                         pipeline_mode=pl.Buffered(1)))
    out_specs = [_token_spec(rows_per_step)]
    out_shape = [jax.ShapeDtypeStruct(x.shape, F32)]
    for w, l in casts:
        _, rows, cols = w.shape
        assert rows % steps == 0
        chunk = rows // steps
        in_specs.append(pl.BlockSpec((None, chunk, cols), lambda b, t, l=l: (l, b * nt + t, 0)))
        out_specs.append(pl.BlockSpec((chunk, cols), lambda b, t: (b * nt + t, 0)))
        out_shape.append(jax.ShapeDtypeStruct((rows, cols), BF16))

    return pl.pallas_call(
        functools.partial(_with_casts, body,
                          tuple(l if a.ndim == 2 else None for a, l in consts), len(casts)),
        grid=(bn, nt),
        in_specs=in_specs,
        out_specs=out_specs,
        out_shape=out_shape,
        scratch_shapes=scratch_shapes,
        compiler_params=_compiler_params(),
        name=name,
    )(x, *[a for a, _ in consts], *[w for w, _ in casts])


def kernel(x, norm_mix, w_in, sgu_ln, sgu_w, sgu_b, conv_b_w, pool_w, pool_scale,
           w_br_a, w_br_b, w_br_c, w_o, norm_ffn, w_up, ffn_conv_w, w_down, norm_f):
    depth = w_in.shape[0]
    ts = SEQ_TILE
    assert x.shape[1] % (ts * SUBTILES) == 0 and ts % SGU_BLOCK == 0
    assert D_FF % FFN_COLS == 0

    sb = jnp.repeat(sgu_b.transpose(0, 2, 1), SGU_HEAD, axis=2)

    mixer_scratch = [pltpu.VMEM((SUBTILES, ts, SGU_WIDTH), BF16),
                     pltpu.VMEM((SUBLANES + ts, CONV_WIDTH), F32),
                     pltpu.VMEM((POOL_HIST + ts, POOL_WIDTH), F32)]
    ffn_scratch = [pltpu.VMEM((SUBTILES, ts, D_FF), BF16),
                   pltpu.VMEM((SUBLANES, 2 * D_FF), F32),
                   pltpu.VMEM((SUBTILES, SHIFT_BUFS, SUBLANES + ts, FFN_COLS), F32)]
    mixer_weights = (w_in, w_br_a, w_br_b, w_br_c, w_o)

    wi, wa, wb, wc, wo = [w[0].astype(BF16) for w in mixer_weights]
    for l in range(depth):
        last = l == depth - 1
        x, wup, wdn = _layer_call(
            _mixer_kernel, "mixer", x,
            [(norm_mix, l), (wi, None), (sgu_ln, l), (sgu_w, l), (sb, l), (conv_b_w, l),
             (pool_w, l), (pool_scale, l), (wa, None), (wb, None), (wc, None), (wo, None)],
            [(w_up, l), (w_down, l)], mixer_scratch)
        x, *next_weights = _layer_call(
            functools.partial(_ffn_kernel, final_norm=last), "ffn", x,
            [(norm_ffn, l), (wup, None), (ffn_conv_w, l), (wdn, None), (norm_f[None], None)],
            [] if last else [(w, l + 1) for w in mixer_weights], ffn_scratch)
        if not last:
            wi, wa, wb, wc, wo = next_weights
    return x
```
